```python
import math
import jax
import jax.numpy as jnp
from jax import lax
import numpy as np


D_MODEL = 2048
BATCH = 8
SEQ = 2048
DEPTH = 4

N_EVEN = (DEPTH + 1) // 2
N_ODD = DEPTH // 2
RMS_EPS = 1e-6

DA_HEADS = 8
DA_QK_DIM = 64
DA_V_DIM = 2 * DA_QK_DIM
DA_WIDTH = DA_HEADS * DA_V_DIM
DA_QK_WIDTH = DA_HEADS * 2 * DA_QK_DIM
DA_QBLOCK = 128

GD_HEADS = 8
GD_DK = 128
GD_DV = 128
GD_WIDTH = GD_HEADS * GD_DV
GD_QKV = 2 * GD_HEADS * GD_DK + GD_WIDTH
GD_CONV = 4
GD_CHUNK = 64

EVEN_MIX = DA_WIDTH + GD_WIDTH
EV_A_END = 2 * DA_QK_WIDTH + 2 * DA_WIDTH
EV_SPLITS = (DA_QK_WIDTH, 2 * DA_QK_WIDTH, 2 * DA_QK_WIDTH + DA_WIDTH, EV_A_END, EV_A_END + GD_QKV, EV_A_END + GD_QKV + GD_WIDTH, EV_A_END + GD_QKV + GD_WIDTH + GD_HEADS)
EVEN_IN = EV_A_END + GD_QKV + GD_WIDTH + 2 * GD_HEADS

SW_HEADS = 32
SW_KV_HEADS = 4
SW_GROUP = SW_HEADS // SW_KV_HEADS
SW_HEAD_DIM = 64
SW_WIDTH = SW_HEADS * SW_HEAD_DIM
SW_KV_WIDTH = SW_KV_HEADS * SW_HEAD_DIM
SW_WINDOW = 128
ODD_SPLITS = (SW_WIDTH, SW_WIDTH + SW_KV_WIDTH, SW_WIDTH + 2 * SW_KV_WIDTH)
ODD_IN = 2 * SW_WIDTH + 2 * SW_KV_WIDTH

kernel_name = 'hybrid_diffattn_gdn_swa_adaln'


def rms_norm(x, w):
    xf = x.astype(jnp.float32)
    y = xf * lax.rsqrt(jnp.mean(xf * xf, axis=-1, keepdims=True) + RMS_EPS)
    return (y * w.astype(jnp.float32)).astype(x.dtype)


def l2_normalize(x):
    xf = x.astype(jnp.float32)
    return (xf * lax.rsqrt(jnp.sum(xf * xf, axis=-1, keepdims=True) + RMS_EPS)).astype(x.dtype)


def alibi_slopes(n):
    return 2.0 ** (-8.0 * jnp.arange(1, n + 1, dtype=jnp.float32) / n)


def causal_depthwise_conv(x, w):
    k_size, ch = w.shape
    return lax.conv_general_dilated(x, w[:, None, :].astype(x.dtype), window_strides=(1,), padding=[(k_size - 1, 0)], dimension_numbers=('NWC', 'WIO', 'NWC'), feature_group_count=ch)


def differential_attention(q, k, v, lam, lam_init, subln_w):
    b, t, h = q.shape[:3]
    scale = DA_QK_DIM ** -0.5
    slopes = alibi_slopes(h)[:, None, None, None]
    outs = []
    for blk in range(t // DA_QBLOCK):
        q0 = blk * DA_QBLOCK
        q1 = q0 + DA_QBLOCK
        dist = (jnp.arange(q0, q1)[:, None] - jnp.arange(q1)[None, :]).astype(jnp.float32)
        s = jnp.einsum('bqhmd,bkhmd->bhmqk', q[:, q0:q1], k[:, :q1]).astype(jnp.float32) * scale
        s = jnp.where(dist >= 0, s - slopes * dist, -jnp.inf)
        p = jax.nn.softmax(s, axis=-1)
        a = p[:, :, 0] - lam * p[:, :, 1]
        outs.append(jnp.einsum('bhqk,bkhd->bqhd', a.astype(v.dtype), v[:, :q1]))
    o = jnp.concatenate(outs, axis=1)
    o = rms_norm(o, subln_w) * (1.0 - lam_init)
    return o.reshape(b, t, h * DA_V_DIM)


def gated_delta_rule(q, k, v, g, beta):
    b, t, h, dk = q.shape
    dv = v.shape[-1]
    cs = GD_CHUNK
    n = t // cs
    f32 = jnp.float32

    def chunks(xs):
        xs = xs.astype(f32).reshape(b, n, cs, h, *xs.shape[3:])
        return jnp.moveaxis(xs, (1, 3), (0, 2))

    qc, kc, vc = chunks(q), chunks(k), chunks(v)
    gc = jnp.cumsum(chunks(g), axis=-1)
    bc = chunks(beta)
    idx = jnp.arange(cs)
    tril = idx[:, None] >= idx[None, :]
    strict = idx[:, None] > idx[None, :]
    decay = jnp.exp(jnp.where(tril, gc[..., :, None] - gc[..., None, :], -jnp.inf))
    kk = jnp.einsum('nbhid,nbhjd->nbhij', kc, kc)
    m_low = jnp.where(strict, bc[..., :, None] * kk * decay, 0.0)
    rhs = jnp.concatenate([bc[..., None] * vc, bc[..., None] * kc * jnp.exp(gc)[..., None]], axis=-1)
    sol = lax.linalg.triangular_solve(m_low + jnp.eye(cs, dtype=f32), rhs, left_side=True, lower=True, unit_diagonal=True)
    u, w = sol[..., :dv], sol[..., dv:]
    qk = jnp.where(tril, jnp.einsum('nbhid,nbhjd->nbhij', qc, kc) * decay, 0.0)

    def step(state, inp):
        q_i, k_i, u_i, w_i, g_i, qk_i = inp
        v_new = u_i - jnp.einsum('bhcd,bhde->bhce', w_i, state)
        o_i = jnp.einsum('bhcd,bhde->bhce', q_i * jnp.exp(g_i)[..., None], state) + jnp.einsum('bhij,bhje->bhie', qk_i, v_new)
        g_last = g_i[..., -1]
        state = state * jnp.exp(g_last)[..., None, None] + jnp.einsum('bhcd,bhce->bhde', k_i * jnp.exp(g_last[..., None] - g_i)[..., None], v_new)
        return state, o_i

    s0 = jnp.zeros((b, h, dk, dv), f32)
    _, o = lax.scan(step, s0, (qc, kc, u, w, gc, qk))
    return jnp.moveaxis(o, (0, 2), (1, 3)).reshape(b, t, h, dv)


def sliding_window_attention(q, k, v, sinks):
    b, t, hkv, grp, d = q.shape
    w = SW_WINDOW
    nb = t // w
    scale = d ** -0.5

    def band(xs):
        xb = xs.reshape(b, nb, w, hkv, d)
        prev = jnp.pad(xb, ((0, 0), (1, 0), (0, 0), (0, 0), (0, 0)))[:, :-1]
        return jnp.concatenate([prev, xb], axis=2)

    kb, vb = band(k), band(v)
    s = jnp.einsum('bnqhgd,bnkhd->bnhgqk', q.reshape(b, nb, w, hkv, grp, d), kb).astype(jnp.float32) * scale
    dist = jnp.arange(w)[:, None] + w - jnp.arange(2 * w)[None, :]
    key_abs = jnp.arange(nb)[:, None, None] * w + jnp.arange(2 * w)[None, None, :] - w
    valid = (dist >= 0) & (dist < w) & (key_abs >= 0)
    slopes = alibi_slopes(hkv * grp).reshape(hkv, grp)[:, :, None, None]
    s = jnp.where(valid[None, :, None, None], s - slopes * dist.astype(jnp.float32), -jnp.inf)
    sink = sinks.astype(jnp.float32).reshape(hkv, grp)[:, :, None, None]
    m = jnp.maximum(jnp.max(s, axis=-1, keepdims=True), sink)
    p = jnp.exp(s - m)
    p = p / (jnp.sum(p, axis=-1, keepdims=True) + jnp.exp(sink - m))
    o = jnp.einsum('bnhgqk,bnkhd->bnqhgd', p.astype(v.dtype), vb)
    return o.reshape(b, t, hkv * grp * d)


def even_mixer(h, w_in, w_out, lam_q1, lam_k1, lam_q2, lam_k2, diff_norm, conv_w, a_log, dt_bias, delta_norm, lam_init):
    b, t, _ = h.shape
    f32 = jnp.float32
    proj = h @ w_in
    a_q, a_k, a_v, a_z, b_qkv, b_z, b_beta, b_a = jnp.split(proj, EV_SPLITS, axis=-1)
    lam = jnp.exp(jnp.sum(lam_q1.astype(f32) * lam_k1.astype(f32))) - jnp.exp(jnp.sum(lam_q2.astype(f32) * lam_k2.astype(f32))) + lam_init
    o_a = differential_attention(a_q.reshape(b, t, DA_HEADS, 2, DA_QK_DIM), a_k.reshape(b, t, DA_HEADS, 2, DA_QK_DIM), a_v.reshape(b, t, DA_HEADS, DA_V_DIM), lam, lam_init, diff_norm)
    qkv = jax.nn.silu(causal_depthwise_conv(b_qkv, conv_w))
    g_q, g_k, g_v = jnp.split(qkv, (GD_HEADS * GD_DK, 2 * GD_HEADS * GD_DK), axis=-1)
    g_q = l2_normalize(g_q.reshape(b, t, GD_HEADS, GD_DK)) * (GD_DK ** -0.5)
    g_k = l2_normalize(g_k.reshape(b, t, GD_HEADS, GD_DK))
    g_v = g_v.reshape(b, t, GD_HEADS, GD_DV)
    beta = jax.nn.sigmoid(b_beta.astype(f32))
    g = -jnp.exp(a_log.astype(f32)) * jax.nn.softplus(b_a.astype(f32) + dt_bias.astype(f32))
    o_b = gated_delta_rule(g_q, g_k, g_v, g, beta).astype(h.dtype)
    o_b = rms_norm(o_b, delta_norm).reshape(b, t, GD_WIDTH)
    y = jnp.concatenate([o_a * jax.nn.silu(a_z), o_b * jax.nn.silu(b_z)], axis=-1)
    return y @ w_out


def odd_mixer(h, w_in, w_out, sinks):
    b, t, _ = h.shape
    proj = h @ w_in
    q, k, v, z = jnp.split(proj, ODD_SPLITS, axis=-1)
    o = sliding_window_attention(q.reshape(b, t, SW_KV_HEADS, SW_GROUP, SW_HEAD_DIM), k.reshape(b, t, SW_KV_HEADS, SW_HEAD_DIM), v.reshape(b, t, SW_KV_HEADS, SW_HEAD_DIM), sinks)
    return (o * jax.nn.silu(z)) @ w_out


def setup_inputs(seed: int = 0) -> dict:
    key = jax.random.key(seed)
    ks = jax.random.split(key, 21)
    f32 = jnp.float32
    d = D_MODEL

    def nrm(k, shape, s):
        return jax.random.normal(k, shape, f32) * s

    dt = jnp.exp(jax.random.uniform(ks[14], (N_EVEN, GD_HEADS), f32, math.log(1e-3), math.log(1e-1)))
    return {
        'x': nrm(ks[0], (BATCH, SEQ, d), 1.0),
        'c': nrm(ks[1], (BATCH, d), 1.0),
        'norm_w': 1.0 + nrm(ks[2], (DEPTH, d), 0.02),
        'ada_w': nrm(ks[3], (DEPTH, d, 3 * d), 0.5 * d ** -0.5),
        'ada_b': nrm(ks[4], (DEPTH, 3 * d), 0.02),
        'ev_w_in': nrm(ks[5], (N_EVEN, d, EVEN_IN), d ** -0.5),
        'ev_w_out': nrm(ks[6], (N_EVEN, EVEN_MIX, d), EVEN_MIX ** -0.5),
        'ev_lam_q1': nrm(ks[7], (N_EVEN, DA_QK_DIM), 0.1),
        'ev_lam_k1': nrm(ks[8], (N_EVEN, DA_QK_DIM), 0.1),
        'ev_lam_q2': nrm(ks[9], (N_EVEN, DA_QK_DIM), 0.1),
        'ev_lam_k2': nrm(ks[10], (N_EVEN, DA_QK_DIM), 0.1),
        'ev_diff_norm': 1.0 + nrm(ks[11], (N_EVEN, DA_V_DIM), 0.02),
        'ev_conv_w': nrm(ks[12], (N_EVEN, GD_CONV, GD_QKV), GD_CONV ** -0.5),
        'ev_A_log': jnp.log(jax.random.uniform(ks[13], (N_EVEN, GD_HEADS), f32, 1.0, 16.0)),
        'ev_dt_bias': dt + jnp.log(-jnp.expm1(-dt)),
        'ev_delta_norm': 1.0 + nrm(ks[15], (N_EVEN, GD_DV), 0.02),
        'od_w_in': nrm(ks[16], (N_ODD, d, ODD_IN), d ** -0.5),
        'od_w_out': nrm(ks[17], (N_ODD, SW_WIDTH, d), SW_WIDTH ** -0.5),
        'od_sinks': nrm(ks[18], (N_ODD, SW_HEADS), 0.5),
        'final_norm_w': 1.0 + nrm(ks[19], (d,), 0.02),
    }


def reference(x, c, norm_w, ada_w, ada_b, ev_w_in, ev_w_out, ev_lam_q1, ev_lam_k1, ev_lam_q2, ev_lam_k2, ev_diff_norm, ev_conv_w, ev_A_log, ev_dt_bias, ev_delta_norm, od_w_in, od_w_out, od_sinks, final_norm_w):
    cond = jax.nn.silu(c)
    for layer in range(DEPTH):
        mod = cond @ ada_w[layer] + ada_b[layer]
        shift, scale, gate = jnp.split(mod[:, None, :], 3, axis=-1)
        h = rms_norm(x, norm_w[layer]) * (1.0 + scale) + shift
        i = layer // 2
        if layer % 2 == 0:
            lam_init = 0.8 - 0.6 * math.exp(-0.3 * layer)
            y = even_mixer(h, ev_w_in[i], ev_w_out[i], ev_lam_q1[i], ev_lam_k1[i], ev_lam_q2[i], ev_lam_k2[i], ev_diff_norm[i], ev_conv_w[i], ev_A_log[i], ev_dt_bias[i], ev_delta_norm[i], lam_init)
        else:
            y = odd_mixer(h, od_w_in[i], od_w_out[i], od_sinks[i])
        x = x + gate * y
    return rms_norm(x, final_norm_w)
```

```python
import functools
import math

import jax
import jax.numpy as jnp
from jax import lax
from jax.experimental import pallas as pl
from jax.experimental.pallas import tpu as pltpu

F32 = jnp.float32
BF16 = jnp.bfloat16

D_MODEL = 2048
DEPTH = 4
RMS_EPS = 1e-6

DA_HEADS = 8
DA_QK_DIM = 64
DA_V_DIM = 128
DA_WIDTH = DA_HEADS * DA_V_DIM

GD_HEADS = 8
GD_DK = 128
GD_DV = 128
GD_WIDTH = GD_HEADS * GD_DV
GD_CONV = 4
GD_CHUNK = 64
GD_QKV = 3 * GD_WIDTH

EVEN_MAIN = 4 * DA_WIDTH + GD_QKV + GD_WIDTH
EVEN_IN = EVEN_MAIN + 2 * GD_HEADS

SW_HEADS = 32
SW_KV_HEADS = 4
SW_GROUP = SW_HEADS // SW_KV_HEADS
SW_HEAD_DIM = 64
SW_WIDTH = SW_HEADS * SW_HEAD_DIM
SW_KV_WIDTH = SW_KV_HEADS * SW_HEAD_DIM
SW_WINDOW = 128
ODD_IN = 2 * SW_WIDTH + 2 * SW_KV_WIDTH

LANES = 128
NEG_BIG = -1e30
VMEM_LIMIT = 56 * 1024 * 1024

HIGHEST = lax.Precision.HIGHEST


def _dot(a, b, precision=None):
    return jnp.dot(a, b, preferred_element_type=F32, precision=precision)


def _dot_nt(a, b, precision=None):
    return lax.dot_general(a, b, (((1,), (1,)), ((), ())), preferred_element_type=F32, precision=precision)


def _dot_tn(a, b, precision=None):
    return lax.dot_general(a, b, (((0,), (0,)), ((), ())), preferred_element_type=F32, precision=precision)


def _silu(x):
    return x * jax.nn.sigmoid(x)


def _alibi_slopes(n):
    return 2.0 ** (-8.0 * jnp.arange(1, n + 1, dtype=F32) / n)


def _ada_kernel(c_ref, w_ref, b_ref, o_ref):
    c = c_ref[...]
    cond = _silu(c).astype(BF16)
    w = w_ref[0].astype(BF16)
    o_ref[0] = _dot(cond, w) + b_ref[0]


def _ada_mod(c, ada_w, ada_b, tn=512):
    depth, d, n3 = ada_w.shape
    b = c.shape[0]
    return pl.pallas_call(
        _ada_kernel,
        grid=(depth, n3 // tn),
        in_specs=[
            pl.BlockSpec((b, d), lambda l, j: (0, 0)),
            pl.BlockSpec((1, d, tn), lambda l, j: (l, 0, j)),
            pl.BlockSpec((1, 1, tn), lambda l, j: (l, 0, j)),
        ],
        out_specs=pl.BlockSpec((1, b, tn), lambda l, j: (l, 0, j)),
        out_shape=jax.ShapeDtypeStruct((depth, b, n3), F32),
        compiler_params=pltpu.CompilerParams(
            dimension_semantics=("arbitrary", "arbitrary"), vmem_limit_bytes=VMEM_LIMIT),
        name="ada_mod",
    )(c, ada_w, ada_b.reshape(depth, 1, n3))


def _inproj_kernel(x_ref, nw_ref, m_ref, w_ref, *rest, has_tail):
    if has_tail:
        wt_ref, o_ref, ot_ref, h_scr = rest
    else:
        o_ref, h_scr = rest

    @pl.when(pl.program_id(1) == 0)
    def _():
        x = x_ref[...]
        ms = jnp.mean(x * x, axis=-1, keepdims=True)
        y = x * lax.rsqrt(ms + RMS_EPS) * nw_ref[...]
        m = m_ref[0]
        h = (y * (1.0 + m[1:2, :]) + m[0:1, :]).astype(BF16)
        h_scr[...] = h
        if has_tail:
            ot_ref[...] = _dot(h, wt_ref[...])

    o_ref[...] = _dot(h_scr[...], w_ref[...]).astype(o_ref.dtype)


def _inproj(x2, norm_w, mod3, w_main, w_tail, seq, tm, tn):
    n, d = x2.shape
    nout = w_main.shape[1]
    per_b = seq // tm
    has_tail = w_tail is not None
    in_specs = [
        pl.BlockSpec((tm, d), lambda i, j: (i, 0)),
        pl.BlockSpec((1, d), lambda i, j: (0, 0)),
        pl.BlockSpec((1, 3, d), lambda i, j: (i // per_b, 0, 0)),
        pl.BlockSpec((d, tn), lambda i, j: (0, j)),
    ]
    args = [x2, norm_w.reshape(1, d), mod3, w_main]
    out_specs = [pl.BlockSpec((tm, tn), lambda i, j: (i, j))]
    out_shape = [jax.ShapeDtypeStruct((n, nout), BF16)]
    if has_tail:
        in_specs.append(pl.BlockSpec((d, LANES), lambda i, j: (0, 0)))
        args.append(w_tail)
        out_specs.append(pl.BlockSpec((tm, LANES), lambda i, j: (i, 0)))
        out_shape.append(jax.ShapeDtypeStruct((n, LANES), F32))
    res = pl.pallas_call(
        functools.partial(_inproj_kernel, has_tail=has_tail),
        grid=(n // tm, nout // tn),
        in_specs=in_specs,
        out_specs=out_specs,
        out_shape=out_shape,
        scratch_shapes=[pltpu.VMEM((tm, d), BF16)],
        compiler_params=pltpu.CompilerParams(
            dimension_semantics=("arbitrary", "arbitrary"), vmem_limit_bytes=VMEM_LIMIT),
        name="inproj_tail" if has_tail else "inproj",
    )(*args)
    return res if has_tail else (res[0], None)


def _outproj_kernel(*refs, n_y, final):
    y_refs = refs[:n_y]
    w_ref, x_ref, m_ref = refs[n_y:n_y + 3]
    if final:
        fw_ref, o_ref = refs[n_y + 3:]
    else:
        (o_ref,) = refs[n_y + 3:]
    acc = None
    row = 0
    for y_ref in y_refs:
        kw = y_ref.shape[1]
        part = _dot(y_ref[...], w_ref[row:row + kw, :])
        acc = part if acc is None else acc + part
        row += kw
    gate = m_ref[0][2:3, :]
    xn = x_ref[...] + gate * acc
    if final:
        ms = jnp.mean(xn * xn, axis=-1, keepdims=True)
        xn = xn * lax.rsqrt(ms + RMS_EPS) * fw_ref[...]
    o_ref[...] = xn


def _outproj(ys, w_out, x2, mod3, final_w, seq, tm):
    n, d = x2.shape
    per_b = seq // tm
    final = final_w is not None
    in_specs = [pl.BlockSpec((tm, y.shape[1]), lambda i: (i, 0)) for y in ys]
    in_specs += [
        pl.BlockSpec(w_out.shape, lambda i: (0, 0)),
        pl.BlockSpec((tm, d), lambda i: (i, 0)),
        pl.BlockSpec((1, 3, d), lambda i: (i // per_b, 0, 0)),
    ]
    args = list(ys) + [w_out, x2, mod3]
    if final:
        in_specs.append(pl.BlockSpec((1, d), lambda i: (0, 0)))
        args.append(final_w.reshape(1, d))
    return pl.pallas_call(
        functools.partial(_outproj_kernel, n_y=len(ys), final=final),
        grid=(n // tm,),
        in_specs=in_specs,
        out_specs=pl.BlockSpec((tm, d), lambda i: (i, 0)),
        out_shape=jax.ShapeDtypeStruct((n, d), F32),
        compiler_params=pltpu.CompilerParams(
            dimension_semantics=("arbitrary",), vmem_limit_bytes=VMEM_LIMIT),
        name="outproj_final" if final else "outproj",
    )(*args)


def _da_kernel(slope_ref, lam_ref, q_ref, k_ref, v_ref, z_ref, nw_ref, o_ref,
               m_scr, l_scr, acc_scr, *, tq, tk, lam_init):
    h = pl.program_id(1)
    qi = pl.program_id(2)
    slope = slope_ref[h]
    i0 = qi * tq

    q = q_ref[...] * jnp.asarray(DA_QK_DIM ** -0.5, BF16)
    lane = lax.broadcasted_iota(jnp.int32, q.shape, 1)
    zero = jnp.zeros_like(q)
    qmaps = (jnp.where(lane < DA_QK_DIM, q, zero), jnp.where(lane >= DA_QK_DIM, q, zero))

    m_scr[...] = jnp.full(m_scr.shape, NEG_BIG, F32)
    l_scr[...] = jnp.zeros(l_scr.shape, F32)
    acc_scr[...] = jnp.zeros(acc_scr.shape, F32)

    col = lax.broadcasted_iota(jnp.int32, (1, tk), 1)

    def tile(ks, masked):
        k = k_ref[pl.ds(ks, tk), :]
        v = v_ref[pl.ds(ks, tk), :]
        bias = slope * (col + (ks - i0)).astype(F32)
        if masked:
            rows = lax.broadcasted_iota(jnp.int32, (tq, tk), 0)
            cols = lax.broadcasted_iota(jnp.int32, (tq, tk), 1)
            keep = cols <= rows
        for mi in range(2):
            s = _dot_nt(qmaps[mi], k) + bias
            if masked:
                s = jnp.where(keep, s, NEG_BIG)
            m_prev = m_scr[mi]
            m_new = jnp.maximum(m_prev, jnp.max(s, axis=-1, keepdims=True))
            alpha = jnp.exp(m_prev - m_new)
            p = jnp.exp(s - m_new)
            l_scr[mi] = alpha * l_scr[mi] + jnp.sum(p, axis=-1, keepdims=True)
            acc_scr[mi] = alpha * acc_scr[mi] + _dot(p.astype(BF16), v)
            m_scr[mi] = m_new

    def body(ki, carry):
        tile(pl.multiple_of(ki * tk, tk), False)
        return carry

    lax.fori_loop(0, qi, body, 0)
    tile(pl.multiple_of(i0, tk), True)

    lv = lam_ref[...]
    lam = (jnp.exp(jnp.sum(lv[0:1] * lv[1:2], axis=-1, keepdims=True))
           - jnp.exp(jnp.sum(lv[2:3] * lv[3:4], axis=-1, keepdims=True)) + lam_init)
    o = acc_scr[0] / l_scr[0] - lam * (acc_scr[1] / l_scr[1])
    ms = jnp.mean(o * o, axis=-1, keepdims=True)
    o = o * lax.rsqrt(ms + RMS_EPS) * nw_ref[...] * (1.0 - lam_init)
    z = z_ref[...].astype(F32)
    o_ref[...] = (o * _silu(z)).astype(o_ref.dtype)


def _diff_attention(proj, lamv, diff_norm, batch, seq, lam_init, tq=512):
    n = proj.shape[0]
    nq = seq // tq
    hq = DA_HEADS
    return pl.pallas_call(
        functools.partial(_da_kernel, tq=tq, tk=tq, lam_init=lam_init),
        grid=(batch, hq, nq),
        in_specs=[
            pl.BlockSpec(memory_space=pltpu.SMEM),
            pl.BlockSpec((4, DA_QK_DIM), lambda b, h, i: (0, 0)),
            pl.BlockSpec((tq, LANES), lambda b, h, i: (b * nq + i, h)),
            pl.BlockSpec((seq, LANES), lambda b, h, i: (b, hq + h)),
            pl.BlockSpec((seq, LANES), lambda b, h, i: (b, 2 * hq + h)),
            pl.BlockSpec((tq, LANES), lambda b, h, i: (b * nq + i, 3 * hq + h)),
            pl.BlockSpec((1, DA_V_DIM), lambda b, h, i: (0, 0)),
        ],
        out_specs=pl.BlockSpec((tq, LANES), lambda b, h, i: (b * nq + i, h)),
        out_shape=jax.ShapeDtypeStruct((n, DA_WIDTH), BF16),
        scratch_shapes=[
            pltpu.VMEM((2, tq, 1), F32),
            pltpu.VMEM((2, tq, 1), F32),
            pltpu.VMEM((2, tq, DA_V_DIM), F32),
        ],
        compiler_params=pltpu.CompilerParams(
            dimension_semantics=("arbitrary", "arbitrary", "arbitrary"), vmem_limit_bytes=VMEM_LIMIT),
        name="diff_attn",
    )(_alibi_slopes(hq), lamv, proj, proj, proj, proj, diff_norm.reshape(1, DA_V_DIM))


GD_HALO = 8


def _gdn_kernel(q_ref, k_ref, v_ref, z_ref, t_ref, cw_ref, alog_ref, dtb_ref, nw_ref, o_ref,
                xbuf, s_scr):
    cs = GD_CHUNK
    c = pl.program_id(1)

    @pl.when(c == 0)
    def _():
        xbuf[0:GD_HALO, :] = jnp.zeros((GD_HALO, GD_QKV), F32)
        s_scr[...] = jnp.zeros(s_scr.shape, F32)

    @pl.when(c > 0)
    def _():
        xbuf[0:GD_HALO, :] = xbuf[cs:cs + GD_HALO, :]

    xbuf[GD_HALO:GD_HALO + cs, 0:GD_WIDTH] = q_ref[...].astype(F32)
    xbuf[GD_HALO:GD_HALO + cs, GD_WIDTH:2 * GD_WIDTH] = k_ref[...].astype(F32)
    xbuf[GD_HALO:GD_HALO + cs, 2 * GD_WIDTH:3 * GD_WIDTH] = v_ref[...].astype(F32)

    def conv_silu(col0):
        acc = None
        for kk in range(GD_CONV):
            r0 = GD_HALO - (GD_CONV - 1) + kk
            term = cw_ref[kk:kk + 1, col0:col0 + LANES] * xbuf[r0:r0 + cs, col0:col0 + LANES]
            acc = term if acc is None else acc + term
        return _silu(acc)

    def l2n(x):
        return x * lax.rsqrt(jnp.sum(x * x, axis=-1, keepdims=True) + RMS_EPS)

    tail = t_ref[...]
    beta_all = jax.nn.sigmoid(tail)
    g_all = -jnp.exp(alog_ref[...]) * jax.nn.softplus(tail + dtb_ref[...])

    ri = lax.broadcasted_iota(jnp.int32, (cs, cs), 0)
    ci = lax.broadcasted_iota(jnp.int32, (cs, cs), 1)
    tril = ri >= ci
    strict = ri > ci
    tril_f = tril.astype(F32)
    triu_f = (ri <= ci).astype(F32)
    ones_f = jnp.ones((cs, cs), F32)
    eye_f = (ri == ci).astype(F32)

    for h in range(GD_HEADS):
        q = l2n(conv_silu(h * GD_DK)) * (GD_DK ** -0.5)
        k = l2n(conv_silu(GD_WIDTH + h * GD_DK))
        v = conv_silu(2 * GD_WIDTH + h * GD_DV)

        bcol = beta_all[:, h:h + 1]
        gcol = g_all[:, GD_HEADS + h:GD_HEADS + h + 1]
        gb = jnp.broadcast_to(gcol, (cs, LANES))
        gc = _dot(tril_f, gb, HIGHEST)
        gc_row = _dot(ones_f, gb[:, :cs] * triu_f, HIGHEST)
        gl = gc[cs - 1:cs, :]
        dmat = gc[:, :cs] - gc_row
        decay = jnp.exp(jnp.where(tril, dmat, NEG_BIG))
        egc = jnp.exp(gc)

        kq = _dot_nt(jnp.concatenate([q, k], axis=0), k, HIGHEST)
        qk = kq[:cs]
        kk_ = kq[cs:]
        m_low = jnp.where(strict, bcol * kk_ * decay, 0.0)
        qkm = jnp.where(tril, qk * decay, 0.0)

        tinv = eye_f - m_low
        mp = m_low
        for _ in range(5):
            mp = _dot(mp, mp, HIGHEST)
            tinv = tinv + _dot(tinv, mp, HIGHEST)

        u = _dot(tinv, bcol * v, HIGHEST)
        w = _dot(tinv, bcol * k * egc, HIGHEST)

        s = s_scr[h]
        v_new = u - _dot(w, s, HIGHEST)
        o = _dot(q * egc, s, HIGHEST) + _dot(qkm, v_new, HIGHEST)
        s_scr[h] = s * jnp.exp(gl) + _dot_tn(k * jnp.exp(gl - gc), v_new, HIGHEST)

        ms = jnp.mean(o * o, axis=-1, keepdims=True)
        o = o * lax.rsqrt(ms + RMS_EPS) * nw_ref[...]
        z = z_ref[:, h * GD_DV:(h + 1) * GD_DV].astype(F32)
        o_ref[:, h * GD_DV:(h + 1) * GD_DV] = (o * _silu(z)).astype(o_ref.dtype)


def _gated_deltanet(proj, tail, conv_w, a_log, dt_bias, delta_norm, batch, seq):
    n = proj.shape[0]
    cs = GD_CHUNK
    nc = seq // cs
    pad = jnp.zeros((GD_HEADS,), F32)
    alog_row = jnp.concatenate([pad, a_log.astype(F32), jnp.zeros((LANES - 2 * GD_HEADS,), F32)]).reshape(1, LANES)
    dtb_row = jnp.concatenate([pad, dt_bias.astype(F32), jnp.zeros((LANES - 2 * GD_HEADS,), F32)]).reshape(1, LANES)
    cb = (4 * DA_WIDTH) // GD_WIDTH
    return pl.pallas_call(
        _gdn_kernel,
        grid=(batch, nc),
        in_specs=[
            pl.BlockSpec((cs, GD_WIDTH), lambda b, c: (b * nc + c, cb)),
            pl.BlockSpec((cs, GD_WIDTH), lambda b, c: (b * nc + c, cb + 1)),
            pl.BlockSpec((cs, GD_WIDTH), lambda b, c: (b * nc + c, cb + 2)),
            pl.BlockSpec((cs, GD_WIDTH), lambda b, c: (b * nc + c, cb + 3)),
            pl.BlockSpec((cs, LANES), lambda b, c: (b * nc + c, 0)),
            pl.BlockSpec((GD_CONV, GD_QKV), lambda b, c: (0, 0)),
            pl.BlockSpec((1, LANES), lambda b, c: (0, 0)),
            pl.BlockSpec((1, LANES), lambda b, c: (0, 0)),
            pl.BlockSpec((1, GD_DV), lambda b, c: (0, 0)),
        ],
        out_specs=pl.BlockSpec((cs, GD_WIDTH), lambda b, c: (b * nc + c, 0)),
        out_shape=jax.ShapeDtypeStruct((n, GD_WIDTH), BF16),
        scratch_shapes=[
            pltpu.VMEM((GD_HALO + cs, GD_QKV), F32),
            pltpu.VMEM((GD_HEADS, GD_DK, GD_DV), F32),
        ],
        compiler_params=pltpu.CompilerParams(
            dimension_semantics=("arbitrary", "arbitrary"), vmem_limit_bytes=VMEM_LIMIT),
        name="gated_deltanet",
    )(proj, proj, proj, proj, tail, conv_w, alog_row, dtb_row, delta_norm.reshape(1, GD_DV))


def _swa_kernel(slope_ref, sink_ref, q_ref, kc_ref, kp_ref, vc_ref, vp_ref,
                z0_ref, z1_ref, z2_ref, z3_ref, o_ref):
    w = SW_WINDOW
    hd = SW_HEAD_DIM
    nblk = pl.program_id(1)
    z_refs = (z0_ref, z1_ref, z2_ref, z3_ref)

    kband = jnp.concatenate([kp_ref[...], kc_ref[...]], axis=0).astype(F32)
    vband = jnp.concatenate([vp_ref[...], vc_ref[...]], axis=0).astype(F32)

    qi = lax.broadcasted_iota(jnp.int32, (w, 2 * w), 0)
    kj = lax.broadcasted_iota(jnp.int32, (w, 2 * w), 1)
    dist = qi + w - kj
    valid = (dist >= 0) & (dist < w) & ((kj >= w) | (nblk > 0))
    dist_f = dist.astype(F32)
    lane = lax.broadcasted_iota(jnp.int32, (2 * w, LANES), 1)
    lo = lane < hd
    slabs_per_kv = SW_GROUP * hd // LANES

    for hk in range(SW_KV_HEADS):
        g0 = (hk // 2) * LANES
        kg = kband[:, g0:g0 + LANES]
        vg = vband[:, g0:g0 + LANES]
        kr = pltpu.roll(kg, hd, 1)
        vr = pltpu.roll(vg, hd, 1)
        if hk % 2 == 0:
            k_lo, k_hi, v_lo, v_hi = kg, kr, vg, vr
        else:
            k_lo, k_hi, v_lo, v_hi = kr, kg, vr, vg
        k_lo = jnp.where(lo, k_lo, 0.0).astype(BF16)
        k_hi = jnp.where(lo, 0.0, k_hi).astype(BF16)
        v_lo = jnp.where(lo, v_lo, 0.0).astype(BF16)
        v_hi = jnp.where(lo, 0.0, v_hi).astype(BF16)
        for sl in range(slabs_per_kv):
            c0 = (hk * slabs_per_kv + sl) * LANES
            qs = q_ref[:, c0:c0 + LANES] * jnp.asarray(hd ** -0.5, BF16)
            outs = []
            for half, (kz, vz) in enumerate(((k_lo, v_lo), (k_hi, v_hi))):
                head = hk * SW_GROUP + 2 * sl + half
                slope = slope_ref[head]
                sink = sink_ref[head]
                s = _dot_nt(qs, kz)
                s = jnp.where(valid, s - slope * dist_f, NEG_BIG)
                m = jnp.maximum(jnp.max(s, axis=-1, keepdims=True), sink)
                p = jnp.exp(s - m)
                den = jnp.sum(p, axis=-1, keepdims=True) + jnp.exp(sink - m)
                outs.append(_dot(p.astype(BF16), vz) / den)
            o = outs[0] + outs[1]
            z = z_refs[hk][:, sl * LANES:(sl + 1) * LANES].astype(F32)
            o_ref[:, c0:c0 + LANES] = (o * _silu(z)).astype(o_ref.dtype)


def _sliding_window_attention(proj, sinks, batch, seq):
    n = proj.shape[0]
    w = SW_WINDOW
    nb = seq // w
    kcb = SW_WIDTH // SW_KV_WIDTH
    zw = SW_WIDTH // SW_KV_HEADS
    zcb = (SW_WIDTH + 2 * SW_KV_WIDTH) // zw
    cur = lambda b, i: b * nb + i
    prev = lambda b, i: b * nb + jnp.maximum(i - 1, 0)
    in_specs = [
        pl.BlockSpec(memory_space=pltpu.SMEM),
        pl.BlockSpec(memory_space=pltpu.SMEM),
        pl.BlockSpec((w, SW_WIDTH), lambda b, i: (cur(b, i), 0)),
        pl.BlockSpec((w, SW_KV_WIDTH), lambda b, i: (cur(b, i), kcb)),
        pl.BlockSpec((w, SW_KV_WIDTH), lambda b, i: (prev(b, i), kcb)),
        pl.BlockSpec((w, SW_KV_WIDTH), lambda b, i: (cur(b, i), kcb + 1)),
        pl.BlockSpec((w, SW_KV_WIDTH), lambda b, i: (prev(b, i), kcb + 1)),
    ]
    for hk in range(SW_KV_HEADS):
        in_specs.append(pl.BlockSpec((w, zw), lambda b, i, hk=hk: (cur(b, i), zcb + hk)))
    return pl.pallas_call(
        _swa_kernel,
        grid=(batch, nb),
        in_specs=in_specs,
        out_specs=pl.BlockSpec((w, SW_WIDTH), lambda b, i: (cur(b, i), 0)),
        out_shape=jax.ShapeDtypeStruct((n, SW_WIDTH), BF16),
        compiler_params=pltpu.CompilerParams(
            dimension_semantics=("arbitrary", "arbitrary"), vmem_limit_bytes=VMEM_LIMIT),
        name="swa",
    )(_alibi_slopes(SW_HEADS), sinks.astype(F32), *([proj] * 9))


def kernel(x, c, norm_w, ada_w, ada_b, ev_w_in, ev_w_out, ev_lam_q1, ev_lam_k1, ev_lam_q2, ev_lam_k2, ev_diff_norm, ev_conv_w, ev_A_log, ev_dt_bias, ev_delta_norm, od_w_in, od_w_out, od_sinks, final_norm_w):
    batch, seq, d = x.shape
    depth = ada_w.shape[0]
    n = batch * seq
    x2 = x.reshape(n, d)

    mod = _ada_mod(c, ada_w, ada_b)
    tm = min(512, seq)

    for layer in range(depth):
        i = layer // 2
        mod3 = mod[layer].reshape(batch, 3, d)
        final_w = final_norm_w if layer == depth - 1 else None
        if layer % 2 == 0:
            lam_init = 0.8 - 0.6 * math.exp(-0.3 * layer)
            w_main = ev_w_in[i][:, :EVEN_MAIN].astype(BF16)
            w_tail = jnp.pad(ev_w_in[i][:, EVEN_MAIN:], ((0, 0), (0, LANES - 2 * GD_HEADS))).astype(BF16)
            proj, tail = _inproj(x2, norm_w[layer], mod3, w_main, w_tail, seq, tm, 1024)
            lamv = jnp.stack([ev_lam_q1[i], ev_lam_k1[i], ev_lam_q2[i], ev_lam_k2[i]]).astype(F32)
            ya = _diff_attention(proj, lamv, ev_diff_norm[i], batch, seq, lam_init, tq=min(512, seq))
            yb = _gated_deltanet(proj, tail, ev_conv_w[i], ev_A_log[i], ev_dt_bias[i], ev_delta_norm[i], batch, seq)
            x2 = _outproj([ya, yb], ev_w_out[i].astype(BF16), x2, mod3, final_w, seq, tm)
        else:
            proj, _ = _inproj(x2, norm_w[layer], mod3, od_w_in[i].astype(BF16), None, seq, tm, 1536)
            y = _sliding_window_attention(proj, od_sinks[i], batch, seq)
            x2 = _outproj([y], od_w_out[i].astype(BF16), x2, mod3, final_w, seq, tm)
    return x2.reshape(batch, seq, d)
```

```python
import functools
import math

import jax
import jax.numpy as jnp
from jax import lax
from jax.experimental import pallas as pl
from jax.experimental.pallas import tpu as pltpu

F32 = jnp.float32
BF16 = jnp.bfloat16

D_MODEL = 2048
DEPTH = 4
RMS_EPS = 1e-6

DA_HEADS = 8
DA_QK_DIM = 64
DA_V_DIM = 128
DA_WIDTH = DA_HEADS * DA_V_DIM

GD_HEADS = 8
GD_DK = 128
GD_DV = 128
GD_WIDTH = GD_HEADS * GD_DV
GD_CONV = 4
GD_CHUNK = 64
GD_QKV = 3 * GD_WIDTH

EVEN_MAIN = 4 * DA_WIDTH + GD_QKV + GD_WIDTH
EVEN_IN = EVEN_MAIN + 2 * GD_HEADS

SW_HEADS = 32
SW_KV_HEADS = 4
SW_GROUP = SW_HEADS // SW_KV_HEADS
SW_HEAD_DIM = 64
SW_WIDTH = SW_HEADS * SW_HEAD_DIM
SW_KV_WIDTH = SW_KV_HEADS * SW_HEAD_DIM
SW_WINDOW = 128
ODD_IN = 2 * SW_WIDTH + 2 * SW_KV_WIDTH

LANES = 128
NEG_BIG = -1e30
VMEM_LIMIT = 56 * 1024 * 1024

HIGHEST = lax.Precision.HIGHEST


def _dot(a, b, precision=None):
    return jnp.dot(a, b, preferred_element_type=F32, precision=precision)


def _dot_nt(a, b, precision=None):
    return lax.dot_general(a, b, (((1,), (1,)), ((), ())), preferred_element_type=F32, precision=precision)


def _dot_tn(a, b, precision=None):
    return lax.dot_general(a, b, (((0,), (0,)), ((), ())), preferred_element_type=F32, precision=precision)


def _silu(x):
    return x * jax.nn.sigmoid(x)


def _alibi_slopes(n):
    return 2.0 ** (-8.0 * jnp.arange(1, n + 1, dtype=F32) / n)


def _ada_kernel(c_ref, w_ref, b_ref, o_ref):
    c = c_ref[...]
    cond = _silu(c).astype(BF16)
    w = w_ref[0].astype(BF16)
    o_ref[0] = _dot(cond, w) + b_ref[0]


def _ada_mod(c, ada_w, ada_b, tn=512):
    depth, d, n3 = ada_w.shape
    b = c.shape[0]
    return pl.pallas_call(
        _ada_kernel,
        grid=(depth, n3 // tn),
        in_specs=[
            pl.BlockSpec((b, d), lambda l, j: (0, 0)),
            pl.BlockSpec((1, d, tn), lambda l, j: (l, 0, j)),
            pl.BlockSpec((1, 1, tn), lambda l, j: (l, 0, j)),
        ],
        out_specs=pl.BlockSpec((1, b, tn), lambda l, j: (l, 0, j)),
        out_shape=jax.ShapeDtypeStruct((depth, b, n3), F32),
        compiler_params=pltpu.CompilerParams(
            dimension_semantics=("arbitrary", "arbitrary"), vmem_limit_bytes=VMEM_LIMIT),
        name="ada_mod",
    )(c, ada_w, ada_b.reshape(depth, 1, n3))


def _modulated_norm(x, nw, m):
    ms = jnp.mean(x * x, axis=-1, keepdims=True)
    y = x * lax.rsqrt(ms + RMS_EPS) * nw
    return (y * (1.0 + m[1:2, :]) + m[0:1, :]).astype(BF16)


def _prenorm_kernel(x_ref, nw_ref, m_ref, h_ref):
    h_ref[...] = _modulated_norm(x_ref[...], nw_ref[...], m_ref[0])


def _prenorm(x2, norm_w, mod3, seq, tm):
    n, d = x2.shape
    per_b = seq // tm
    return pl.pallas_call(
        _prenorm_kernel,
        grid=(n // tm,),
        in_specs=[
            pl.BlockSpec((tm, d), lambda i: (i, 0)),
            pl.BlockSpec((1, d), lambda i: (0, 0)),
            pl.BlockSpec((1, 3, d), lambda i: (i // per_b, 0, 0)),
        ],
        out_specs=pl.BlockSpec((tm, d), lambda i: (i, 0)),
        out_shape=jax.ShapeDtypeStruct((n, d), BF16),
        compiler_params=pltpu.CompilerParams(
            dimension_semantics=("arbitrary",), vmem_limit_bytes=VMEM_LIMIT),
        name="prenorm",
    )(x2, norm_w.reshape(1, d), mod3)


def _inproj_kernel(h_ref, w_ref, *rest, has_tail):
    if has_tail:
        wt_ref, o_ref, ot_ref = rest

        @pl.when(pl.program_id(1) == 0)
        def _():
            ot_ref[...] = _dot(h_ref[...], wt_ref[...])
    else:
        (o_ref,) = rest
    o_ref[...] = _dot(h_ref[...], w_ref[...]).astype(o_ref.dtype)


def _inproj(h2, w_main, w_tail, tm, tn):
    n, d = h2.shape
    nout = w_main.shape[1]
    has_tail = w_tail is not None
    in_specs = [
        pl.BlockSpec((tm, d), lambda i, j: (i, 0)),
        pl.BlockSpec((d, tn), lambda i, j: (0, j)),
    ]
    args = [h2, w_main]
    out_specs = [pl.BlockSpec((tm, tn), lambda i, j: (i, j))]
    out_shape = [jax.ShapeDtypeStruct((n, nout), BF16)]
    if has_tail:
        in_specs.append(pl.BlockSpec((d, LANES), lambda i, j: (0, 0)))
        args.append(w_tail)
        out_specs.append(pl.BlockSpec((tm, LANES), lambda i, j: (i, 0)))
        out_shape.append(jax.ShapeDtypeStruct((n, LANES), F32))
    res = pl.pallas_call(
        functools.partial(_inproj_kernel, has_tail=has_tail),
        grid=(n // tm, nout // tn),
        in_specs=in_specs,
        out_specs=out_specs,
        out_shape=out_shape,
        compiler_params=pltpu.CompilerParams(
            dimension_semantics=("arbitrary", "arbitrary"), vmem_limit_bytes=VMEM_LIMIT),
        name="inproj_tail" if has_tail else "inproj",
    )(*args)
    return res if has_tail else (res[0], None)


def _outproj_kernel(*refs, n_y, final):
    y_refs = refs[:n_y]
    w_ref, x_ref, m_ref, nw_ref = refs[n_y:n_y + 4]
    if final:
        (o_ref,) = refs[n_y + 4:]
    else:
        mn_ref, o_ref, h_ref = refs[n_y + 4:]
    acc = None
    row = 0
    for y_ref in y_refs:
        kw = y_ref.shape[1]
        part = _dot(y_ref[...], w_ref[row:row + kw, :])
        acc = part if acc is None else acc + part
        row += kw
    gate = m_ref[0][2:3, :]
    xn = x_ref[...] + gate * acc
    if final:
        ms = jnp.mean(xn * xn, axis=-1, keepdims=True)
        o_ref[...] = xn * lax.rsqrt(ms + RMS_EPS) * nw_ref[...]
    else:
        o_ref[...] = xn
        h_ref[...] = _modulated_norm(xn, nw_ref[...], mn_ref[0])


def _outproj(ys, w_out, x2, mod3, next_norm_w, next_mod3, seq, tm):
    n, d = x2.shape
    per_b = seq // tm
    final = next_mod3 is None
    row_spec = pl.BlockSpec((tm, d), lambda i: (i, 0))
    mod_spec = pl.BlockSpec((1, 3, d), lambda i: (i // per_b, 0, 0))
    in_specs = [pl.BlockSpec((tm, y.shape[1]), lambda i: (i, 0)) for y in ys]
    in_specs += [pl.BlockSpec(w_out.shape, lambda i: (0, 0)), row_spec, mod_spec,
                 pl.BlockSpec((1, d), lambda i: (0, 0))]
    args = list(ys) + [w_out, x2, mod3, next_norm_w.reshape(1, d)]
    if final:
        out_specs = row_spec
        out_shape = jax.ShapeDtypeStruct((n, d), F32)
    else:
        in_specs.append(mod_spec)
        args.append(next_mod3)
        out_specs = [row_spec, row_spec]
        out_shape = [jax.ShapeDtypeStruct((n, d), F32), jax.ShapeDtypeStruct((n, d), BF16)]
    res = pl.pallas_call(
        functools.partial(_outproj_kernel, n_y=len(ys), final=final),
        grid=(n // tm,),
        in_specs=in_specs,
        out_specs=out_specs,
        out_shape=out_shape,
        compiler_params=pltpu.CompilerParams(
            dimension_semantics=("arbitrary",), vmem_limit_bytes=VMEM_LIMIT),
        name="outproj_final" if final else "outproj",
    )(*args)
    return (res, None) if final else res


def _da_kernel(slope_ref, lam_ref, q_ref, k_ref, v_ref, z_ref, nw_ref, o_ref,
               m_scr, l_scr, acc_scr, *, tq, tk, lam_init):
    h = pl.program_id(1)
    qi = pl.program_id(2)
    slope = slope_ref[h]
    i0 = qi * tq

    q = q_ref[...] * jnp.asarray(DA_QK_DIM ** -0.5, BF16)
    lane = lax.broadcasted_iota(jnp.int32, q.shape, 1)
    zero = jnp.zeros_like(q)
    qst = jnp.concatenate([jnp.where(lane < DA_QK_DIM, q, zero), jnp.where(lane >= DA_QK_DIM, q, zero)], axis=0)

    m_scr[...] = jnp.full(m_scr.shape, NEG_BIG, F32)
    l_scr[...] = jnp.zeros(l_scr.shape, F32)
    acc_scr[...] = jnp.zeros(acc_scr.shape, F32)

    key_i = lax.broadcasted_iota(jnp.int32, (tk, LANES), 0)
    nrep = 2 * tq // LANES

    def tile(ks, masked):
        k = k_ref[pl.ds(ks, tk), :]
        v = v_ref[pl.ds(ks, tk), :]
        bias = slope * (key_i + (ks - i0)).astype(F32)
        s = _dot_nt(k, qst) + jnp.concatenate([bias] * nrep, axis=1)
        if masked:
            kj = lax.broadcasted_iota(jnp.int32, (tk, 2 * tq), 0)
            qc = lax.broadcasted_iota(jnp.int32, (tk, 2 * tq), 1) & (tq - 1)
            s = jnp.where(kj <= qc, s, NEG_BIG)
        m_prev = m_scr[...]
        m_new = jnp.maximum(m_prev, jnp.max(s, axis=0, keepdims=True))
        alpha = jnp.exp(m_prev - m_new)
        p = jnp.exp(s - m_new)
        l_scr[...] = alpha * l_scr[...] + jnp.sum(p, axis=0, keepdims=True)
        acc_scr[...] = alpha * acc_scr[...] + _dot_tn(v, p.astype(BF16))
        m_scr[...] = m_new

    def body(ki, carry):
        tile(pl.multiple_of(ki * tk, tk), False)
        return carry

    lax.fori_loop(0, qi, body, 0)
    tile(pl.multiple_of(i0, tk), True)

    lv = lam_ref[...]
    lam = (jnp.exp(jnp.sum(lv[0:1] * lv[1:2], axis=-1, keepdims=True))
           - jnp.exp(jnp.sum(lv[2:3] * lv[3:4], axis=-1, keepdims=True)) + lam_init)
    acc = acc_scr[...] / l_scr[...]
    o = (acc[:, :tq] - lam * acc[:, tq:]).T
    ms = jnp.mean(o * o, axis=-1, keepdims=True)
    o = o * lax.rsqrt(ms + RMS_EPS) * nw_ref[...] * (1.0 - lam_init)
    z = z_ref[...].astype(F32)
    o_ref[...] = (o * _silu(z)).astype(o_ref.dtype)


def _diff_attention(proj, lamv, diff_norm, batch, seq, lam_init, tq=512):
    n = proj.shape[0]
    nq = seq // tq
    hq = DA_HEADS
    return pl.pallas_call(
        functools.partial(_da_kernel, tq=tq, tk=tq, lam_init=lam_init),
        grid=(batch, hq, nq),
        in_specs=[
            pl.BlockSpec(memory_space=pltpu.SMEM),
            pl.BlockSpec((4, DA_QK_DIM), lambda b, h, i: (0, 0)),
            pl.BlockSpec((tq, LANES), lambda b, h, i: (b * nq + i, h)),
            pl.BlockSpec((seq, LANES), lambda b, h, i: (b, hq + h)),
            pl.BlockSpec((seq, LANES), lambda b, h, i: (b, 2 * hq + h)),
            pl.BlockSpec((tq, LANES), lambda b, h, i: (b * nq + i, 3 * hq + h)),
            pl.BlockSpec((1, DA_V_DIM), lambda b, h, i: (0, 0)),
        ],
        out_specs=pl.BlockSpec((tq, LANES), lambda b, h, i: (b * nq + i, h)),
        out_shape=jax.ShapeDtypeStruct((n, DA_WIDTH), BF16),
        scratch_shapes=[
            pltpu.VMEM((1, 2 * tq), F32),
            pltpu.VMEM((1, 2 * tq), F32),
            pltpu.VMEM((DA_V_DIM, 2 * tq), F32),
        ],
        compiler_params=pltpu.CompilerParams(
            dimension_semantics=("arbitrary", "arbitrary", "arbitrary"), vmem_limit_bytes=VMEM_LIMIT),
        name="diff_attn",
    )(_alibi_slopes(hq), lamv, proj, proj, proj, proj, diff_norm.reshape(1, DA_V_DIM))


GD_HALO = 8
GD_ROWS = 2 * GD_CHUNK


def _bdot(a, b):
    return _dot(a.astype(BF16), b.astype(BF16))


def _split3(x):
    b1 = x.astype(BF16)
    r1 = x - b1.astype(F32)
    b2 = r1.astype(BF16)
    b3 = (r1 - b2.astype(F32)).astype(BF16)
    return b1, b2, b3


def _gdn_kernel(q_ref, k_ref, v_ref, z_ref, t_ref, cw_ref, alog_ref, dtb_ref, nw_ref, o_ref,
                xbuf, s_scr):
    cs = GD_CHUNK
    rows = GD_ROWS
    heads = range(GD_HEADS)
    step = pl.program_id(1)

    @pl.when(step == 0)
    def _():
        xbuf[0:GD_HALO, :] = jnp.zeros((GD_HALO, GD_QKV), F32)
        s_scr[...] = jnp.zeros(s_scr.shape, F32)

    @pl.when(step > 0)
    def _():
        xbuf[0:GD_HALO, :] = xbuf[rows:rows + GD_HALO, :]

    xbuf[GD_HALO:GD_HALO + rows, 0:GD_WIDTH] = q_ref[...].astype(F32)
    xbuf[GD_HALO:GD_HALO + rows, GD_WIDTH:2 * GD_WIDTH] = k_ref[...].astype(F32)
    xbuf[GD_HALO:GD_HALO + rows, 2 * GD_WIDTH:3 * GD_WIDTH] = v_ref[...].astype(F32)

    def conv_silu(col0):
        acc = None
        for kk in range(GD_CONV):
            r0 = GD_HALO - (GD_CONV - 1) + kk
            term = cw_ref[kk:kk + 1, col0:col0 + LANES] * xbuf[r0:r0 + rows, col0:col0 + LANES]
            acc = term if acc is None else acc + term
        return _silu(acc)

    def l2n(x):
        return x * lax.rsqrt(jnp.sum(x * x, axis=-1, keepdims=True) + RMS_EPS)

    tail = t_ref[...]
    beta_all = jax.nn.sigmoid(tail)
    g_all = -jnp.exp(alog_ref[...]) * jax.nn.softplus(tail + dtb_ref[...])

    ri = lax.broadcasted_iota(jnp.int32, (rows, rows), 0)
    ci = lax.broadcasted_iota(jnp.int32, (rows, rows), 1)
    same = (ri // cs) == (ci // cs)
    tril = same & (ri >= ci)
    strict = same & (ri > ci)
    tril_b = jnp.where(tril, 1.0, 0.0).astype(BF16)
    triu_b = jnp.where(same & (ri <= ci), 1.0, 0.0).astype(BF16)
    eye_f = jnp.where(ri == ci, 1.0, 0.0)

    gcs = _dot(tril_b, jnp.concatenate(_split3(g_all), axis=1))
    gc_all = gcs[:, :LANES] + gcs[:, LANES:2 * LANES] + gcs[:, 2 * LANES:]
    gts = _dot(jnp.concatenate(_split3(g_all.T), axis=0), triu_b)
    gc_t = gts[:LANES] + gts[LANES:2 * LANES] + gts[2 * LANES:]

    qs, ks, vs, bcols, gcbs, egcs = [], [], [], [], [], []
    for h in heads:
        qs.append(l2n(conv_silu(h * GD_DK)) * (GD_DK ** -0.5))
        ks.append(l2n(conv_silu(GD_WIDTH + h * GD_DK)))
        vs.append(conv_silu(2 * GD_WIDTH + h * GD_DV))
        bcols.append(beta_all[:, h:h + 1])
        gcb = jnp.broadcast_to(gc_all[:, GD_HEADS + h:GD_HEADS + h + 1], (rows, LANES))
        gcbs.append(gcb)
        egcs.append(jnp.exp(gcb))

    kqs = [_dot_nt(jnp.concatenate([qs[h], ks[h]], axis=0).astype(BF16), ks[h].astype(BF16)) for h in heads]

    m_lows, qkms = [], []
    for h in heads:
        dmat = gcbs[h] - gc_t[GD_HEADS + h:GD_HEADS + h + 1, :]
        decay = jnp.exp(jnp.where(tril, dmat, NEG_BIG))
        m_lows.append(jnp.where(strict, bcols[h] * kqs[h][rows:] * decay, 0.0))
        qkms.append(jnp.where(tril, kqs[h][:rows] * decay, 0.0))

    tinvs = [eye_f - m_lows[h] for h in heads]
    mps = m_lows
    for _ in range(5):
        mps = [_bdot(mps[h], mps[h]) for h in heads]
        tinvs = [tinvs[h] + _bdot(tinvs[h], mps[h]) for h in heads]

    sols = [_bdot(tinvs[h], jnp.concatenate([bcols[h] * vs[h], bcols[h] * ks[h] * egcs[h]], axis=1)) for h in heads]

    states = [s_scr[h] for h in heads]
    v_news = [[] for _ in heads]
    o_inters = [[] for _ in heads]
    for c in range(rows // cs):
        r0 = c * cs
        for h in heads:
            s = states[h]
            gcb_c = gcbs[h][r0:r0 + cs]
            gl = gcb_c[cs - 1:cs, :]
            v_new = sols[h][r0:r0 + cs, :GD_DV] - _bdot(sols[h][r0:r0 + cs, GD_DV:], s)
            o_inters[h].append(_bdot(qs[h][r0:r0 + cs] * egcs[h][r0:r0 + cs], s))
            kd = ks[h][r0:r0 + cs] * jnp.exp(gl - gcb_c)
            states[h] = s * jnp.exp(gl) + _dot_tn(kd.astype(BF16), v_new.astype(BF16))
            v_news[h].append(v_new)

    for h in heads:
        s_scr[h] = states[h]
        o = jnp.concatenate(o_inters[h], axis=0) + _bdot(qkms[h], jnp.concatenate(v_news[h], axis=0))
        ms = jnp.mean(o * o, axis=-1, keepdims=True)
        o = o * lax.rsqrt(ms + RMS_EPS) * nw_ref[...]
        z = z_ref[:, h * GD_DV:(h + 1) * GD_DV].astype(F32)
        o_ref[:, h * GD_DV:(h + 1) * GD_DV] = (o * _silu(z)).astype(o_ref.dtype)


def _gated_deltanet(proj, tail, conv_w, a_log, dt_bias, delta_norm, batch, seq):
    n = proj.shape[0]
    cs = GD_ROWS
    nc = seq // cs
    pad = jnp.zeros((GD_HEADS,), F32)
    alog_row = jnp.concatenate([pad, a_log.astype(F32), jnp.zeros((LANES - 2 * GD_HEADS,), F32)]).reshape(1, LANES)
    dtb_row = jnp.concatenate([pad, dt_bias.astype(F32), jnp.zeros((LANES - 2 * GD_HEADS,), F32)]).reshape(1, LANES)
    cb = (4 * DA_WIDTH) // GD_WIDTH
    return pl.pallas_call(
        _gdn_kernel,
        grid=(batch, nc),
        in_specs=[
            pl.BlockSpec((cs, GD_WIDTH), lambda b, c: (b * nc + c, cb)),
            pl.BlockSpec((cs, GD_WIDTH), lambda b, c: (b * nc + c, cb + 1)),
            pl.BlockSpec((cs, GD_WIDTH), lambda b, c: (b * nc + c, cb + 2)),
            pl.BlockSpec((cs, GD_WIDTH), lambda b, c: (b * nc + c, cb + 3)),
            pl.BlockSpec((cs, LANES), lambda b, c: (b * nc + c, 0)),
            pl.BlockSpec((GD_CONV, GD_QKV), lambda b, c: (0, 0)),
            pl.BlockSpec((1, LANES), lambda b, c: (0, 0)),
            pl.BlockSpec((1, LANES), lambda b, c: (0, 0)),
            pl.BlockSpec((1, GD_DV), lambda b, c: (0, 0)),
        ],
        out_specs=pl.BlockSpec((cs, GD_WIDTH), lambda b, c: (b * nc + c, 0)),
        out_shape=jax.ShapeDtypeStruct((n, GD_WIDTH), BF16),
        scratch_shapes=[
            pltpu.VMEM((GD_HALO + cs, GD_QKV), F32),
            pltpu.VMEM((GD_HEADS, GD_DK, GD_DV), F32),
        ],
        compiler_params=pltpu.CompilerParams(
            dimension_semantics=("arbitrary", "arbitrary"), vmem_limit_bytes=VMEM_LIMIT),
        name="gated_deltanet",
    )(proj, proj, proj, proj, tail, conv_w, alog_row, dtb_row, delta_norm.reshape(1, GD_DV))


def _swa_kernel(slope_ref, sink_ref, q_ref, kc_ref, kp_ref, vc_ref, vp_ref,
                z0_ref, z1_ref, z2_ref, z3_ref, o_ref):
    w = SW_WINDOW
    hd = SW_HEAD_DIM
    nblk = pl.program_id(1)
    z_refs = (z0_ref, z1_ref, z2_ref, z3_ref)

    kband = jnp.concatenate([kp_ref[...], kc_ref[...]], axis=0).astype(F32)
    vband = jnp.concatenate([vp_ref[...], vc_ref[...]], axis=0).astype(F32)

    qi = lax.broadcasted_iota(jnp.int32, (w, 2 * w), 0)
    kj = lax.broadcasted_iota(jnp.int32, (w, 2 * w), 1)
    dist = qi + w - kj
    valid = (dist >= 0) & (dist < w) & ((kj >= w) | (nblk > 0))
    dist_f = dist.astype(F32)
    lane = lax.broadcasted_iota(jnp.int32, (2 * w, LANES), 1)
    lo = lane < hd
    slabs_per_kv = SW_GROUP * hd // LANES

    for hk in range(SW_KV_HEADS):
        g0 = (hk // 2) * LANES
        kg = kband[:, g0:g0 + LANES]
        vg = vband[:, g0:g0 + LANES]
        kr = pltpu.roll(kg, hd, 1)
        vr = pltpu.roll(vg, hd, 1)
        if hk % 2 == 0:
            k_lo, k_hi, v_lo, v_hi = kg, kr, vg, vr
        else:
            k_lo, k_hi, v_lo, v_hi = kr, kg, vr, vg
        k_lo = jnp.where(lo, k_lo, 0.0).astype(BF16)
        k_hi = jnp.where(lo, 0.0, k_hi).astype(BF16)
        v_lo = jnp.where(lo, v_lo, 0.0).astype(BF16)
        v_hi = jnp.where(lo, 0.0, v_hi).astype(BF16)
        for sl in range(slabs_per_kv):
            c0 = (hk * slabs_per_kv + sl) * LANES
            qs = q_ref[:, c0:c0 + LANES] * jnp.asarray(hd ** -0.5, BF16)
            outs = []
            for half, (kz, vz) in enumerate(((k_lo, v_lo), (k_hi, v_hi))):
                head = hk * SW_GROUP + 2 * sl + half
                slope = slope_ref[head]
                sink = sink_ref[head]
                s = _dot_nt(qs, kz)
                s = jnp.where(valid, s - slope * dist_f, NEG_BIG)
                m = jnp.maximum(jnp.max(s, axis=-1, keepdims=True), sink)
                p = jnp.exp(s - m)
                den = jnp.sum(p, axis=-1, keepdims=True) + jnp.exp(sink - m)
                outs.append(_dot(p.astype(BF16), vz) / den)
            o = outs[0] + outs[1]
            z = z_refs[hk][:, sl * LANES:(sl + 1) * LANES].astype(F32)
            o_ref[:, c0:c0 + LANES] = (o * _silu(z)).astype(o_ref.dtype)


def _sliding_window_attention(proj, sinks, batch, seq):
    n = proj.shape[0]
    w = SW_WINDOW
    nb = seq // w
    kcb = SW_WIDTH // SW_KV_WIDTH
    zw = SW_WIDTH // SW_KV_HEADS
    zcb = (SW_WIDTH + 2 * SW_KV_WIDTH) // zw
    cur = lambda b, i: b * nb + i
    prev = lambda b, i: b * nb + jnp.maximum(i - 1, 0)
    in_specs = [
        pl.BlockSpec(memory_space=pltpu.SMEM),
        pl.BlockSpec(memory_space=pltpu.SMEM),
        pl.BlockSpec((w, SW_WIDTH), lambda b, i: (cur(b, i), 0)),
        pl.BlockSpec((w, SW_KV_WIDTH), lambda b, i: (cur(b, i), kcb)),
        pl.BlockSpec((w, SW_KV_WIDTH), lambda b, i: (prev(b, i), kcb)),
        pl.BlockSpec((w, SW_KV_WIDTH), lambda b, i: (cur(b, i), kcb + 1)),
        pl.BlockSpec((w, SW_KV_WIDTH), lambda b, i: (prev(b, i), kcb + 1)),
    ]
    for hk in range(SW_KV_HEADS):
        in_specs.append(pl.BlockSpec((w, zw), lambda b, i, hk=hk: (cur(b, i), zcb + hk)))
    return pl.pallas_call(
        _swa_kernel,
        grid=(batch, nb),
        in_specs=in_specs,
        out_specs=pl.BlockSpec((w, SW_WIDTH), lambda b, i: (cur(b, i), 0)),
        out_shape=jax.ShapeDtypeStruct((n, SW_WIDTH), BF16),
        compiler_params=pltpu.CompilerParams(
            dimension_semantics=("arbitrary", "arbitrary"), vmem_limit_bytes=VMEM_LIMIT),
        name="swa",
    )(_alibi_slopes(SW_HEADS), sinks.astype(F32), *([proj] * 9))


def kernel(x, c, norm_w, ada_w, ada_b, ev_w_in, ev_w_out, ev_lam_q1, ev_lam_k1, ev_lam_q2, ev_lam_k2, ev_diff_norm, ev_conv_w, ev_A_log, ev_dt_bias, ev_delta_norm, od_w_in, od_w_out, od_sinks, final_norm_w):
    batch, seq, d = x.shape
    depth = ada_w.shape[0]
    n = batch * seq
    x2 = x.reshape(n, d)

    mod = _ada_mod(c, ada_w, ada_b)
    tm_out = min(512, seq)
    tm_in = min(1024, seq)
    mod3s = [mod[layer].reshape(batch, 3, d) for layer in range(depth)]

    h2 = _prenorm(x2, norm_w[0], mod3s[0], seq, tm_out)
    for layer in range(depth):
        i = layer // 2
        last = layer == depth - 1
        next_norm_w = final_norm_w if last else norm_w[layer + 1]
        next_mod3 = None if last else mod3s[layer + 1]
        if layer % 2 == 0:
            lam_init = 0.8 - 0.6 * math.exp(-0.3 * layer)
            w_main = ev_w_in[i][:, :EVEN_MAIN].astype(BF16)
            w_tail = jnp.pad(ev_w_in[i][:, EVEN_MAIN:], ((0, 0), (0, LANES - 2 * GD_HEADS))).astype(BF16)
            proj, tail = _inproj(h2, w_main, w_tail, tm_in, 1024)
            lamv = jnp.stack([ev_lam_q1[i], ev_lam_k1[i], ev_lam_q2[i], ev_lam_k2[i]]).astype(F32)
            ya = _diff_attention(proj, lamv, ev_diff_norm[i], batch, seq, lam_init, tq=min(512, seq))
            yb = _gated_deltanet(proj, tail, ev_conv_w[i], ev_A_log[i], ev_dt_bias[i], ev_delta_norm[i], batch, seq)
            ys, w_out = [ya, yb], ev_w_out[i]
        else:
            proj, _ = _inproj(h2, od_w_in[i].astype(BF16), None, tm_in, 1536)
            ys, w_out = [_sliding_window_attention(proj, od_sinks[i], batch, seq)], od_w_out[i]
        x2, h2 = _outproj(ys, w_out.astype(BF16), x2, mod3s[layer], next_norm_w, next_mod3, seq, tm_out)
    return x2.reshape(batch, seq, d)
```

```python
import functools
import math

import jax
import jax.numpy as jnp
from jax import lax
from jax.experimental import pallas as pl
from jax.experimental.pallas import tpu as pltpu

F32 = jnp.float32
BF16 = jnp.bfloat16

D_MODEL = 2048
DEPTH = 4
RMS_EPS = 1e-6

DA_HEADS = 8
DA_QK_DIM = 64
DA_V_DIM = 128
DA_WIDTH = DA_HEADS * DA_V_DIM

GD_HEADS = 8
GD_DK = 128
GD_DV = 128
GD_WIDTH = GD_HEADS * GD_DV
GD_CONV = 4
GD_CHUNK = 64
GD_QKV = 3 * GD_WIDTH

EVEN_MAIN = 4 * DA_WIDTH + GD_QKV + GD_WIDTH
EVEN_IN = EVEN_MAIN + 2 * GD_HEADS

SW_HEADS = 32
SW_KV_HEADS = 4
SW_GROUP = SW_HEADS // SW_KV_HEADS
SW_HEAD_DIM = 64
SW_WIDTH = SW_HEADS * SW_HEAD_DIM
SW_KV_WIDTH = SW_KV_HEADS * SW_HEAD_DIM
SW_WINDOW = 128
ODD_IN = 2 * SW_WIDTH + 2 * SW_KV_WIDTH

LANES = 128
NEG_BIG = -1e30
LOG2E = math.log2(math.e)
VMEM_LIMIT = 56 * 1024 * 1024

HIGHEST = lax.Precision.HIGHEST


def _dot(a, b, precision=None):
    return jnp.dot(a, b, preferred_element_type=F32, precision=precision)


def _dot_nt(a, b, precision=None):
    return lax.dot_general(a, b, (((1,), (1,)), ((), ())), preferred_element_type=F32, precision=precision)


def _dot_tn(a, b, precision=None):
    return lax.dot_general(a, b, (((0,), (0,)), ((), ())), preferred_element_type=F32, precision=precision)


def _silu(x):
    return x * jax.nn.sigmoid(x)


def _alibi_slopes(n):
    return 2.0 ** (-8.0 * jnp.arange(1, n + 1, dtype=F32) / n)


def _ada_kernel(c_ref, w_ref, b_ref, o_ref):
    c = c_ref[...]
    cond = _silu(c).astype(BF16)
    w = w_ref[0].astype(BF16)
    o_ref[0] = _dot(cond, w) + b_ref[0]


def _ada_mod(c, ada_w, ada_b, tn=512):
    depth, d, n3 = ada_w.shape
    b = c.shape[0]
    return pl.pallas_call(
        _ada_kernel,
        grid=(depth, n3 // tn),
        in_specs=[
            pl.BlockSpec((b, d), lambda l, j: (0, 0)),
            pl.BlockSpec((1, d, tn), lambda l, j: (l, 0, j)),
            pl.BlockSpec((1, 1, tn), lambda l, j: (l, 0, j)),
        ],
        out_specs=pl.BlockSpec((1, b, tn), lambda l, j: (l, 0, j)),
        out_shape=jax.ShapeDtypeStruct((depth, b, n3), F32),
        compiler_params=pltpu.CompilerParams(
            dimension_semantics=("arbitrary", "arbitrary"), vmem_limit_bytes=VMEM_LIMIT),
        name="ada_mod",
    )(c, ada_w, ada_b.reshape(depth, 1, n3))


def _modulated_norm(x, nw, m):
    ms = jnp.mean(x * x, axis=-1, keepdims=True)
    y = x * lax.rsqrt(ms + RMS_EPS) * nw
    return (y * (1.0 + m[1:2, :]) + m[0:1, :]).astype(BF16)


def _prenorm_kernel(x_ref, nw_ref, m_ref, wt_ref, h_ref, t_ref):
    h = _modulated_norm(x_ref[...], nw_ref[...], m_ref[0])
    h_ref[...] = h
    t_ref[...] = _dot(h, wt_ref[...])


def _prenorm(x2, norm_w, mod3, w_tail, seq, tm):
    n, d = x2.shape
    per_b = seq // tm
    return pl.pallas_call(
        _prenorm_kernel,
        grid=(n // tm,),
        in_specs=[
            pl.BlockSpec((tm, d), lambda i: (i, 0)),
            pl.BlockSpec((1, d), lambda i: (0, 0)),
            pl.BlockSpec((1, 3, d), lambda i: (i // per_b, 0, 0)),
            pl.BlockSpec((d, LANES), lambda i: (0, 0)),
        ],
        out_specs=[pl.BlockSpec((tm, d), lambda i: (i, 0)), pl.BlockSpec((tm, LANES), lambda i: (i, 0))],
        out_shape=[jax.ShapeDtypeStruct((n, d), BF16), jax.ShapeDtypeStruct((n, LANES), F32)],
        compiler_params=pltpu.CompilerParams(
            dimension_semantics=("arbitrary",), vmem_limit_bytes=VMEM_LIMIT),
        name="prenorm",
    )(x2, norm_w.reshape(1, d), mod3, w_tail)


def _inproj_kernel(h_ref, w_ref, o_ref, wb_scr):
    @pl.when(pl.program_id(1) == 0)
    def _():
        wb_scr[...] = w_ref[0].astype(BF16)

    o_ref[...] = _dot(h_ref[...], wb_scr[...]).astype(o_ref.dtype)


def _inproj(h2, w_all, layer_idx, nout, tm, tn):
    n, d = h2.shape
    return pl.pallas_call(
        _inproj_kernel,
        grid=(nout // tn, n // tm),
        in_specs=[
            pl.BlockSpec((tm, d), lambda j, i: (i, 0)),
            pl.BlockSpec((1, d, tn), lambda j, i: (layer_idx, 0, j)),
        ],
        out_specs=pl.BlockSpec((tm, tn), lambda j, i: (i, j)),
        out_shape=jax.ShapeDtypeStruct((n, nout), BF16),
        scratch_shapes=[pltpu.VMEM((d, tn), BF16)],
        compiler_params=pltpu.CompilerParams(
            dimension_semantics=("arbitrary", "arbitrary"), vmem_limit_bytes=VMEM_LIMIT),
        name="inproj",
    )(h2, w_all)


def _outproj_kernel(*refs, n_y, final, has_tail):
    y_refs = refs[:n_y]
    w_ref, x_ref, m_ref, nw_ref = refs[n_y:n_y + 4]
    rest = list(refs[n_y + 4:])
    wb_scr = rest.pop()
    if not final:
        mn_ref = rest.pop(0)
        wt_ref = rest.pop(0) if has_tail else None
    o_ref = rest.pop(0)

    @pl.when(pl.program_id(0) == 0)
    def _():
        wb_scr[...] = w_ref[0].astype(BF16)

    acc = None
    row = 0
    for y_ref in y_refs:
        kw = y_ref.shape[1]
        part = _dot(y_ref[...], wb_scr[row:row + kw, :])
        acc = part if acc is None else acc + part
        row += kw
    gate = m_ref[0][2:3, :]
    xn = x_ref[...] + gate * acc
    if final:
        ms = jnp.mean(xn * xn, axis=-1, keepdims=True)
        o_ref[...] = xn * lax.rsqrt(ms + RMS_EPS) * nw_ref[...]
    else:
        o_ref[...] = xn
        h = _modulated_norm(xn, nw_ref[...], mn_ref[0])
        rest[0][...] = h
        if has_tail:
            rest[1][...] = _dot(h, wt_ref[...])


def _outproj(ys, w_all, layer_idx, x2, mod3, next_norm_w, next_mod3, next_w_tail, seq, tm):
    n, d = x2.shape
    per_b = seq // tm
    final = next_mod3 is None
    has_tail = next_w_tail is not None
    row_spec = pl.BlockSpec((tm, d), lambda i: (i, 0))
    mod_spec = pl.BlockSpec((1, 3, d), lambda i: (i // per_b, 0, 0))
    wshape = (1,) + w_all.shape[1:]
    in_specs = [pl.BlockSpec((tm, y.shape[1]), lambda i: (i, 0)) for y in ys]
    in_specs += [pl.BlockSpec(wshape, lambda i: (layer_idx, 0, 0), pipeline_mode=pl.Buffered(1)),
                 row_spec, mod_spec, pl.BlockSpec((1, d), lambda i: (0, 0))]
    args = list(ys) + [w_all, x2, mod3, next_norm_w.reshape(1, d)]
    if final:
        out_specs = row_spec
        out_shape = jax.ShapeDtypeStruct((n, d), F32)
    else:
        in_specs.append(mod_spec)
        args.append(next_mod3)
        out_specs = [row_spec, row_spec]
        out_shape = [jax.ShapeDtypeStruct((n, d), F32), jax.ShapeDtypeStruct((n, d), BF16)]
        if has_tail:
            in_specs.append(pl.BlockSpec((d, LANES), lambda i: (0, 0)))
            args.append(next_w_tail)
            out_specs.append(pl.BlockSpec((tm, LANES), lambda i: (i, 0)))
            out_shape.append(jax.ShapeDtypeStruct((n, LANES), F32))
    res = pl.pallas_call(
        functools.partial(_outproj_kernel, n_y=len(ys), final=final, has_tail=has_tail),
        grid=(n // tm,),
        in_specs=in_specs,
        out_specs=out_specs,
        out_shape=out_shape,
        scratch_shapes=[pltpu.VMEM(w_all.shape[1:], BF16)],
        compiler_params=pltpu.CompilerParams(
            dimension_semantics=("arbitrary",), vmem_limit_bytes=VMEM_LIMIT),
        name="outproj_final" if final else ("outproj_tail" if has_tail else "outproj"),
    )(*args)
    if final:
        return res, None, None
    return (res[0], res[1], res[2]) if has_tail else (res[0], res[1], None)


def _da_kernel(slope_ref, lam_ref, q_ref, k_ref, v_ref, z_ref, nw_ref, o_ref,
               m_scr, l_scr, acc_scr, *, tq, tk, lam_init):
    h = pl.program_id(1)
    qi = pl.program_id(2)
    slope = slope_ref[h]
    i0 = qi * tq

    q = q_ref[...].astype(F32) * (DA_QK_DIM ** -0.5 * LOG2E)
    lane = lax.broadcasted_iota(jnp.int32, q.shape, 1)
    qmaps = (jnp.where(lane < DA_QK_DIM, q, 0.0).astype(BF16), jnp.where(lane >= DA_QK_DIM, q, 0.0).astype(BF16))
    slope2 = slope * LOG2E

    m_scr[...] = jnp.full(m_scr.shape, NEG_BIG, F32)
    l_scr[...] = jnp.zeros(l_scr.shape, F32)
    acc_scr[...] = jnp.zeros(acc_scr.shape, F32)

    def tile(ks, nk, q0, masked):
        nq = tq - q0
        cols = ((q0, tq), (tq + q0, 2 * tq))
        k = k_ref[pl.ds(ks, nk), :]
        v = v_ref[pl.ds(ks, nk), :]
        qst = jnp.concatenate([qm[q0:] for qm in qmaps], axis=0)
        key_i = lax.broadcasted_iota(jnp.int32, (nk, LANES), 0)
        bias = slope2 * (key_i + (ks - i0)).astype(F32)
        s = _dot_nt(k, qst) + jnp.concatenate([bias] * (2 * nq // LANES), axis=1)
        if masked:
            kj = lax.broadcasted_iota(jnp.int32, (nk, 2 * nq), 0) + (ks - i0)
            qc = (lax.broadcasted_iota(jnp.int32, (nk, 2 * nq), 1) & (nq - 1)) + q0
            s = jnp.where(kj <= qc, s, NEG_BIG)
        gather = lambda ref: jnp.concatenate([ref[:, a:b] for a, b in cols], axis=1)
        m_prev = gather(m_scr)
        m_new = jnp.maximum(m_prev, jnp.max(s, axis=0, keepdims=True))
        alpha = jnp.exp2(m_prev - m_new)
        p = jnp.exp2(s - m_new)
        l_new = alpha * gather(l_scr) + jnp.sum(p, axis=0, keepdims=True)
        acc_new = alpha * gather(acc_scr) + _dot_tn(v, p.astype(BF16))
        for ci, (a, b) in enumerate(cols):
            m_scr[:, a:b] = m_new[:, ci * nq:(ci + 1) * nq]
            l_scr[:, a:b] = l_new[:, ci * nq:(ci + 1) * nq]
            acc_scr[:, a:b] = acc_new[:, ci * nq:(ci + 1) * nq]

    def body(ki, carry):
        tile(pl.multiple_of(ki * tk, tk), tk, 0, False)
        return carry

    lax.fori_loop(0, qi, body, 0)
    half = tk // 2
    tile(pl.multiple_of(i0, tk), half, 0, True)
    tile(pl.multiple_of(i0 + half, half), half, tq // 2, True)

    lv = lam_ref[...]
    lam = (jnp.exp(jnp.sum(lv[0:1] * lv[1:2], axis=-1, keepdims=True))
           - jnp.exp(jnp.sum(lv[2:3] * lv[3:4], axis=-1, keepdims=True)) + lam_init)
    acc = acc_scr[...] / l_scr[...]
    o = (acc[:, :tq] - lam * acc[:, tq:]).T
    ms = jnp.mean(o * o, axis=-1, keepdims=True)
    o = o * lax.rsqrt(ms + RMS_EPS) * nw_ref[...] * (1.0 - lam_init)
    z = z_ref[...].astype(F32)
    o_ref[...] = (o * _silu(z)).astype(o_ref.dtype)


def _diff_attention(proj, lamv, diff_norm, batch, seq, lam_init, tq=512):
    n = proj.shape[0]
    nq = seq // tq
    hq = DA_HEADS
    return pl.pallas_call(
        functools.partial(_da_kernel, tq=tq, tk=tq, lam_init=lam_init),
        grid=(batch, hq, nq),
        in_specs=[
            pl.BlockSpec(memory_space=pltpu.SMEM),
            pl.BlockSpec((4, DA_QK_DIM), lambda b, h, i: (0, 0)),
            pl.BlockSpec((tq, LANES), lambda b, h, i: (b * nq + i, h)),
            pl.BlockSpec((seq, LANES), lambda b, h, i: (b, hq + h)),
            pl.BlockSpec((seq, LANES), lambda b, h, i: (b, 2 * hq + h)),
            pl.BlockSpec((tq, LANES), lambda b, h, i: (b * nq + i, 3 * hq + h)),
            pl.BlockSpec((1, DA_V_DIM), lambda b, h, i: (0, 0)),
        ],
        out_specs=pl.BlockSpec((tq, LANES), lambda b, h, i: (b * nq + i, h)),
        out_shape=jax.ShapeDtypeStruct((n, DA_WIDTH), BF16),
        scratch_shapes=[
            pltpu.VMEM((1, 2 * tq), F32),
            pltpu.VMEM((1, 2 * tq), F32),
            pltpu.VMEM((DA_V_DIM, 2 * tq), F32),
        ],
        compiler_params=pltpu.CompilerParams(
            dimension_semantics=("arbitrary", "arbitrary", "arbitrary"), vmem_limit_bytes=VMEM_LIMIT),
        name="diff_attn",
    )(_alibi_slopes(hq), lamv, proj, proj, proj, proj, diff_norm.reshape(1, DA_V_DIM))


GD_HALO = 8
GD_ROWS = 2 * GD_CHUNK


def _bdot(a, b):
    return _dot(a.astype(BF16), b.astype(BF16))


def _split3(x):
    b1 = x.astype(BF16)
    r1 = x - b1.astype(F32)
    b2 = r1.astype(BF16)
    b3 = (r1 - b2.astype(F32)).astype(BF16)
    return b1, b2, b3


def _gdn_kernel(q_ref, k_ref, v_ref, z_ref, t_ref, cw_ref, alog_ref, dtb_ref, nw_ref, o_ref,
                xbuf, s_scr):
    cs = GD_CHUNK
    rows = GD_ROWS
    heads = range(GD_HEADS)
    step = pl.program_id(1)

    @pl.when(step == 0)
    def _():
        xbuf[0:GD_HALO, :] = jnp.zeros((GD_HALO, GD_QKV), F32)
        s_scr[...] = jnp.zeros(s_scr.shape, F32)

    @pl.when(step > 0)
    def _():
        xbuf[0:GD_HALO, :] = xbuf[rows:rows + GD_HALO, :]

    xbuf[GD_HALO:GD_HALO + rows, 0:GD_WIDTH] = q_ref[...].astype(F32)
    xbuf[GD_HALO:GD_HALO + rows, GD_WIDTH:2 * GD_WIDTH] = k_ref[...].astype(F32)
    xbuf[GD_HALO:GD_HALO + rows, 2 * GD_WIDTH:3 * GD_WIDTH] = v_ref[...].astype(F32)

    def conv_silu(col0):
        xs = xbuf[:, col0:col0 + LANES]
        acc = None
        for kk in range(GD_CONV):
            shift = GD_CONV - 1 - kk
            xk = xs if shift == 0 else pltpu.roll(xs, shift, 0)
            term = cw_ref[kk:kk + 1, col0:col0 + LANES] * xk[GD_HALO:]
            acc = term if acc is None else acc + term
        return _silu(acc)

    def l2n(x):
        return x * lax.rsqrt(jnp.sum(x * x, axis=-1, keepdims=True) + RMS_EPS)

    tail = t_ref[...]
    beta_all = jax.nn.sigmoid(tail)
    g_all = -jnp.exp(alog_ref[...]) * jax.nn.softplus(tail + dtb_ref[...])

    ri = lax.broadcasted_iota(jnp.int32, (rows, rows), 0)
    ci = lax.broadcasted_iota(jnp.int32, (rows, rows), 1)
    same = (ri // cs) == (ci // cs)
    tril = same & (ri >= ci)
    strict = same & (ri > ci)
    tril_b = jnp.where(tril, 1.0, 0.0).astype(BF16)
    triu_b = jnp.where(same & (ri <= ci), 1.0, 0.0).astype(BF16)
    eye_f = jnp.where(ri == ci, 1.0, 0.0)

    gcs = _dot(tril_b, jnp.concatenate(_split3(g_all), axis=1))
    gc_all = gcs[:, :LANES] + gcs[:, LANES:2 * LANES] + gcs[:, 2 * LANES:]
    gts = _dot(jnp.concatenate(_split3(g_all.T), axis=0), triu_b)
    gc_t = gts[:LANES] + gts[LANES:2 * LANES] + gts[2 * LANES:]

    qs, ks, vs, bcols, gcbs, egcs = [], [], [], [], [], []
    for h in heads:
        qs.append(l2n(conv_silu(h * GD_DK)) * (GD_DK ** -0.5))
        ks.append(l2n(conv_silu(GD_WIDTH + h * GD_DK)))
        vs.append(conv_silu(2 * GD_WIDTH + h * GD_DV))
        bcols.append(beta_all[:, h:h + 1])
        gcb = jnp.broadcast_to(gc_all[:, GD_HEADS + h:GD_HEADS + h + 1], (rows, LANES))
        gcbs.append(gcb)
        egcs.append(jnp.exp(gcb))

    kqs = [_dot_nt(jnp.concatenate([qs[h], ks[h]], axis=0).astype(BF16), ks[h].astype(BF16)) for h in heads]

    m_lows, qkms = [], []
    for h in heads:
        dmat = gcbs[h] - gc_t[GD_HEADS + h:GD_HEADS + h + 1, :]
        decay = jnp.exp(jnp.where(tril, dmat, NEG_BIG))
        m_lows.append(jnp.where(strict, bcols[h] * kqs[h][rows:] * decay, 0.0))
        qkms.append(jnp.where(tril, kqs[h][:rows] * decay, 0.0))

    blk = lambda sz: (ri // sz) == (ci // sz)
    tinvs = [eye_f - jnp.where(blk(2), m_lows[h], 0.0) for h in heads]
    sz = 4
    while sz <= cs:
        join = blk(sz) & jnp.logical_not(blk(sz // 2))
        tcs = [_bdot(tinvs[h], jnp.where(join, m_lows[h], 0.0)) for h in heads]
        tinvs = [tinvs[h] - _bdot(tcs[h], tinvs[h]) for h in heads]
        sz *= 2

    sols = [_bdot(tinvs[h], jnp.concatenate([bcols[h] * vs[h], bcols[h] * ks[h] * egcs[h]], axis=1)) for h in heads]

    states = [s_scr[h] for h in heads]
    v_news = [[] for _ in heads]
    o_inters = [[] for _ in heads]
    for c in range(rows // cs):
        r0 = c * cs
        for h in heads:
            s = states[h]
            gcb_c = gcbs[h][r0:r0 + cs]
            gl = gcb_c[cs - 1:cs, :]
            v_new = sols[h][r0:r0 + cs, :GD_DV] - _bdot(sols[h][r0:r0 + cs, GD_DV:], s)
            o_inters[h].append(_bdot(qs[h][r0:r0 + cs] * egcs[h][r0:r0 + cs], s))
            kd = ks[h][r0:r0 + cs] * jnp.exp(gl - gcb_c)
            states[h] = s * jnp.exp(gl) + _dot_tn(kd.astype(BF16), v_new.astype(BF16))
            v_news[h].append(v_new)

    for h in heads:
        s_scr[h] = states[h]
        o = jnp.concatenate(o_inters[h], axis=0) + _bdot(qkms[h], jnp.concatenate(v_news[h], axis=0))
        ms = jnp.mean(o * o, axis=-1, keepdims=True)
        o = o * lax.rsqrt(ms + RMS_EPS) * nw_ref[...]
        z = z_ref[:, h * GD_DV:(h + 1) * GD_DV].astype(F32)
        o_ref[:, h * GD_DV:(h + 1) * GD_DV] = (o * _silu(z)).astype(o_ref.dtype)


def _gated_deltanet(proj, tail, conv_w, a_log, dt_bias, delta_norm, batch, seq):
    n = proj.shape[0]
    cs = GD_ROWS
    nc = seq // cs
    pad = jnp.zeros((GD_HEADS,), F32)
    alog_row = jnp.concatenate([pad, a_log.astype(F32), jnp.zeros((LANES - 2 * GD_HEADS,), F32)]).reshape(1, LANES)
    dtb_row = jnp.concatenate([pad, dt_bias.astype(F32), jnp.zeros((LANES - 2 * GD_HEADS,), F32)]).reshape(1, LANES)
    cb = (4 * DA_WIDTH) // GD_WIDTH
    return pl.pallas_call(
        _gdn_kernel,
        grid=(batch, nc),
        in_specs=[
            pl.BlockSpec((cs, GD_WIDTH), lambda b, c: (b * nc + c, cb)),
            pl.BlockSpec((cs, GD_WIDTH), lambda b, c: (b * nc + c, cb + 1)),
            pl.BlockSpec((cs, GD_WIDTH), lambda b, c: (b * nc + c, cb + 2)),
            pl.BlockSpec((cs, GD_WIDTH), lambda b, c: (b * nc + c, cb + 3)),
            pl.BlockSpec((cs, LANES), lambda b, c: (b * nc + c, 0)),
            pl.BlockSpec((GD_CONV, GD_QKV), lambda b, c: (0, 0)),
            pl.BlockSpec((1, LANES), lambda b, c: (0, 0)),
            pl.BlockSpec((1, LANES), lambda b, c: (0, 0)),
            pl.BlockSpec((1, GD_DV), lambda b, c: (0, 0)),
        ],
        out_specs=pl.BlockSpec((cs, GD_WIDTH), lambda b, c: (b * nc + c, 0)),
        out_shape=jax.ShapeDtypeStruct((n, GD_WIDTH), BF16),
        scratch_shapes=[
            pltpu.VMEM((GD_HALO + cs, GD_QKV), F32),
            pltpu.VMEM((GD_HEADS, GD_DK, GD_DV), F32),
        ],
        compiler_params=pltpu.CompilerParams(
            dimension_semantics=("arbitrary", "arbitrary"), vmem_limit_bytes=VMEM_LIMIT),
        name="gated_deltanet",
    )(proj, proj, proj, proj, tail, conv_w, alog_row, dtb_row, delta_norm.reshape(1, GD_DV))


def _swa_bias_table(nheads):
    w = SW_WINDOW
    qi = jnp.arange(w)[:, None]
    kj = jnp.arange(2 * w)[None, :]
    dist = qi + w - kj
    in_window = (dist >= 0) & (dist < w)
    bias = -_alibi_slopes(nheads)[:, None, None] * dist.astype(F32)[None] * LOG2E
    general = jnp.where(in_window[None], bias, NEG_BIG)
    first = jnp.where((in_window & (kj >= w))[None], bias, NEG_BIG)
    return jnp.stack([first, general])


def _swa_kernel(sink_ref, bias_ref, q_ref, kc_ref, kp_ref, vc_ref, vp_ref,
                z0_ref, z1_ref, z2_ref, z3_ref, o_ref):
    w = SW_WINDOW
    hd = SW_HEAD_DIM
    z_refs = (z0_ref, z1_ref, z2_ref, z3_ref)

    kband = jnp.concatenate([kp_ref[...], kc_ref[...]], axis=0).astype(F32)
    vband = jnp.concatenate([vp_ref[...], vc_ref[...]], axis=0).astype(F32)

    lane = lax.broadcasted_iota(jnp.int32, (2 * w, LANES), 1)
    lo = lane < hd
    slabs = range(SW_GROUP * hd // LANES)
    kvs = range(SW_KV_HEADS)
    halves = range(2)
    ones_b = jnp.ones((2 * w, LANES), BF16)

    q4s, kzs, vxs = [], [], []
    for hk in kvs:
        g0 = (hk // 2) * LANES
        kg = kband[:, g0:g0 + LANES]
        vg = vband[:, g0:g0 + LANES]
        kr = pltpu.roll(kg, hd, 1)
        vr = pltpu.roll(vg, hd, 1)
        if hk % 2 == 0:
            k_lo, k_hi, v_lo, v_hi = kg, kr, vg, vr
        else:
            k_lo, k_hi, v_lo, v_hi = kr, kg, vr, vg
        kzs.append((jnp.where(lo, k_lo, 0.0).astype(BF16), jnp.where(lo, 0.0, k_hi).astype(BF16)))
        vxs.append((jnp.concatenate([jnp.where(lo, v_lo, 0.0).astype(BF16), ones_b], axis=1),
                    jnp.concatenate([jnp.where(lo, 0.0, v_hi).astype(BF16), ones_b], axis=1)))
        q4 = jnp.concatenate([q_ref[:, (hk * len(slabs) + sl) * LANES:(hk * len(slabs) + sl + 1) * LANES]
                              for sl in slabs], axis=0)
        q4s.append((q4.astype(F32) * (hd ** -0.5 * LOG2E)).astype(BF16))

    def qk(hk):
        return [_dot_nt(q4s[hk], kzs[hk][half]) for half in halves]

    scores = {0: qk(0)}
    for hk in kvs:
        if hk + 1 < SW_KV_HEADS:
            scores[hk + 1] = qk(hk + 1)
        o = None
        for half in halves:
            ps, es = [], []
            for sl in slabs:
                head = hk * SW_GROUP + 2 * sl + half
                sink = sink_ref[head] * LOG2E
                s = scores[hk][half][sl * w:(sl + 1) * w] + bias_ref[0, head]
                m = jnp.maximum(jnp.max(s, axis=-1, keepdims=True), sink)
                ps.append(jnp.exp2(s - m).astype(BF16))
                es.append(jnp.exp2(sink - m))
            r = _dot(jnp.concatenate(ps, axis=0), vxs[hk][half])
            part = r[:, :LANES] / (r[:, LANES:] + jnp.concatenate(es, axis=0))
            o = part if o is None else o + part
        for sl in slabs:
            c0 = (hk * len(slabs) + sl) * LANES
            z = z_refs[hk][:, sl * LANES:(sl + 1) * LANES].astype(F32)
            o_ref[:, c0:c0 + LANES] = (o[sl * w:(sl + 1) * w] * _silu(z)).astype(o_ref.dtype)


def _sliding_window_attention(proj, sinks, batch, seq):
    n = proj.shape[0]
    w = SW_WINDOW
    nb = seq // w
    kcb = SW_WIDTH // SW_KV_WIDTH
    zw = SW_WIDTH // SW_KV_HEADS
    zcb = (SW_WIDTH + 2 * SW_KV_WIDTH) // zw
    cur = lambda b, i: b * nb + i
    prev = lambda b, i: b * nb + jnp.maximum(i - 1, 0)
    in_specs = [
        pl.BlockSpec(memory_space=pltpu.SMEM),
        pl.BlockSpec((1, SW_HEADS, w, 2 * w), lambda b, i: (jnp.minimum(i, 1), 0, 0, 0)),
        pl.BlockSpec((w, SW_WIDTH), lambda b, i: (cur(b, i), 0)),
        pl.BlockSpec((w, SW_KV_WIDTH), lambda b, i: (cur(b, i), kcb)),
        pl.BlockSpec((w, SW_KV_WIDTH), lambda b, i: (prev(b, i), kcb)),
        pl.BlockSpec((w, SW_KV_WIDTH), lambda b, i: (cur(b, i), kcb + 1)),
        pl.BlockSpec((w, SW_KV_WIDTH), lambda b, i: (prev(b, i), kcb + 1)),
    ]
    for hk in range(SW_KV_HEADS):
        in_specs.append(pl.BlockSpec((w, zw), lambda b, i, hk=hk: (cur(b, i), zcb + hk)))
    return pl.pallas_call(
        _swa_kernel,
        grid=(batch, nb),
        in_specs=in_specs,
        out_specs=pl.BlockSpec((w, SW_WIDTH), lambda b, i: (cur(b, i), 0)),
        out_shape=jax.ShapeDtypeStruct((n, SW_WIDTH), BF16),
        compiler_params=pltpu.CompilerParams(
            dimension_semantics=("arbitrary", "arbitrary"), vmem_limit_bytes=VMEM_LIMIT),
        name="swa",
    )(sinks.astype(F32), _swa_bias_table(SW_HEADS), *([proj] * 9))


def kernel(x, c, norm_w, ada_w, ada_b, ev_w_in, ev_w_out, ev_lam_q1, ev_lam_k1, ev_lam_q2, ev_lam_k2, ev_diff_norm, ev_conv_w, ev_A_log, ev_dt_bias, ev_delta_norm, od_w_in, od_w_out, od_sinks, final_norm_w):
    batch, seq, d = x.shape
    depth = ada_w.shape[0]
    n = batch * seq
    x2 = x.reshape(n, d)

    mod = _ada_mod(c, ada_w, ada_b)
    tm_out = min(512, seq)
    tm_in = min(1024, seq)
    mod3s = [mod[layer].reshape(batch, 3, d) for layer in range(depth)]

    def tail_weights(i):
        return jnp.pad(ev_w_in[i][:, EVEN_MAIN:], ((0, 0), (0, LANES - 2 * GD_HEADS))).astype(BF16)

    h2, tail = _prenorm(x2, norm_w[0], mod3s[0], tail_weights(0), seq, tm_out)
    for layer in range(depth):
        i = layer // 2
        last = layer == depth - 1
        next_norm_w = final_norm_w if last else norm_w[layer + 1]
        next_mod3 = None if last else mod3s[layer + 1]
        next_w_tail = tail_weights((layer + 1) // 2) if (not last and layer % 2 == 1) else None
        if layer % 2 == 0:
            lam_init = 0.8 - 0.6 * math.exp(-0.3 * layer)
            proj = _inproj(h2, ev_w_in, i, EVEN_MAIN, tm_in, 1024)
            lamv = jnp.stack([ev_lam_q1[i], ev_lam_k1[i], ev_lam_q2[i], ev_lam_k2[i]]).astype(F32)
            ya = _diff_attention(proj, lamv, ev_diff_norm[i], batch, seq, lam_init, tq=min(512, seq))
            yb = _gated_deltanet(proj, tail, ev_conv_w[i], ev_A_log[i], ev_dt_bias[i], ev_delta_norm[i], batch, seq)
            ys, w_out = [ya, yb], ev_w_out
        else:
            proj = _inproj(h2, od_w_in, i, ODD_IN, tm_in, 1536)
            ys, w_out = [_sliding_window_attention(proj, od_sinks[i], batch, seq)], od_w_out
        x2, h2, tail = _outproj(ys, w_out, i, x2, mod3s[layer], next_norm_w, next_mod3, next_w_tail, seq, tm_out)
    return x2.reshape(batch, seq, d)
```

```python
import functools
import math

import jax
import jax.numpy as jnp
from jax import lax
from jax.experimental import pallas as pl
from jax.experimental.pallas import tpu as pltpu

F32 = jnp.float32
BF16 = jnp.bfloat16

D_MODEL = 2048
DEPTH = 4
RMS_EPS = 1e-6

DA_HEADS = 8
DA_QK_DIM = 64
DA_V_DIM = 128
DA_WIDTH = DA_HEADS * DA_V_DIM
GD_HEADS = 8
GD_DK = 128
GD_DV = 128
GD_WIDTH = GD_HEADS * GD_DV
GD_CONV = 4
GD_CHUNK = 64
GD_QKV = 3 * GD_WIDTH

EVEN_MAIN = 4 * DA_WIDTH + GD_QKV + GD_WIDTH
EVEN_IN = EVEN_MAIN + 2 * GD_HEADS

SW_HEADS = 32
SW_KV_HEADS = 4
SW_GROUP = SW_HEADS // SW_KV_HEADS
SW_HEAD_DIM = 64
SW_WIDTH = SW_HEADS * SW_HEAD_DIM
SW_KV_WIDTH = SW_KV_HEADS * SW_HEAD_DIM
SW_WINDOW = 128
ODD_IN = 2 * SW_WIDTH + 2 * SW_KV_WIDTH

LANES = 128
NEG_BIG = -1e30
LOG2E = math.log2(math.e)
OUT_ROW_CHUNKS = 2
VMEM_LIMIT = 56 * 1024 * 1024

HIGHEST = lax.Precision.HIGHEST


def _dot(a, b, precision=None):
    return jnp.dot(a, b, preferred_element_type=F32, precision=precision)


def _dot_nt(a, b, precision=None):
    return lax.dot_general(a, b, (((1,), (1,)), ((), ())), preferred_element_type=F32, precision=precision)


def _dot_tn(a, b, precision=None):
    return lax.dot_general(a, b, (((0,), (0,)), ((), ())), preferred_element_type=F32, precision=precision)


def _silu(x):
    return x * jax.nn.sigmoid(x)


def _alibi_slopes(n):
    return 2.0 ** (-8.0 * jnp.arange(1, n + 1, dtype=F32) / n)


def _ada_kernel(c_ref, w_ref, b_ref, o_ref):
    c = c_ref[...]
    cond = _silu(c).astype(BF16)
    w = w_ref[0].astype(BF16)
    o_ref[0] = _dot(cond, w) + b_ref[0]


def _ada_mod(c, ada_w, ada_b, tn=512):
    depth, d, n3 = ada_w.shape
    b = c.shape[0]
    return pl.pallas_call(
        _ada_kernel,
        grid=(depth, n3 // tn),
        in_specs=[
            pl.BlockSpec((b, d), lambda l, j: (0, 0)),
            pl.BlockSpec((1, d, tn), lambda l, j: (l, 0, j)),
            pl.BlockSpec((1, 1, tn), lambda l, j: (l, 0, j)),
        ],
        out_specs=pl.BlockSpec((1, b, tn), lambda l, j: (l, 0, j)),
        out_shape=jax.ShapeDtypeStruct((depth, b, n3), F32),
        compiler_params=pltpu.CompilerParams(
            dimension_semantics=("arbitrary", "arbitrary"), vmem_limit_bytes=VMEM_LIMIT),
        name="ada_mod",
    )(c, ada_w, ada_b.reshape(depth, 1, n3))


def _modulated_norm(x, nw, m):
    ms = jnp.mean(x * x, axis=-1, keepdims=True)
    y = x * lax.rsqrt(ms + RMS_EPS) * nw
    return (y * (1.0 + m[1:2, :]) + m[0:1, :]).astype(BF16)


def _tail_proj(h, wt_ref):
    w = wt_ref[0]
    row = lax.broadcasted_iota(jnp.int32, w.shape, 0)
    return _dot_nt(h, jnp.where(row < 2 * GD_HEADS, w, 0.0).astype(BF16))


def _tail_spec(layer_idx, d):
    return pl.BlockSpec((1, LANES, d), lambda i: (layer_idx, EVEN_MAIN // LANES, 0))


def _prenorm_kernel(x_ref, nw_ref, m_ref, wt_ref, h_ref, t_ref):
    h = _modulated_norm(x_ref[...], nw_ref[...], m_ref[0])
    h_ref[...] = h
    t_ref[...] = _tail_proj(h, wt_ref)


def _prenorm(x2, norm_w, mod3, ev_w_in, layer_idx, seq, tm):
    n, d = x2.shape
    per_b = seq // tm
    return pl.pallas_call(
        _prenorm_kernel,
        grid=(n // tm,),
        in_specs=[
            pl.BlockSpec((tm, d), lambda i: (i, 0)),
            pl.BlockSpec((1, d), lambda i: (0, 0)),
            pl.BlockSpec((1, 3, d), lambda i: (i // per_b, 0, 0)),
            _tail_spec(layer_idx, d),
        ],
        out_specs=[pl.BlockSpec((tm, d), lambda i: (i, 0)), pl.BlockSpec((tm, LANES), lambda i: (i, 0))],
        out_shape=[jax.ShapeDtypeStruct((n, d), BF16), jax.ShapeDtypeStruct((n, LANES), F32)],
        compiler_params=pltpu.CompilerParams(
            dimension_semantics=("arbitrary",), vmem_limit_bytes=VMEM_LIMIT),
        name="prenorm",
    )(x2, norm_w.reshape(1, d), mod3, ev_w_in)


def _inproj_kernel(h_ref, w_ref, o_ref, wb_scr, *, w_transposed):
    @pl.when(pl.program_id(1) == 0)
    def _():
        wb_scr[...] = w_ref[0].astype(BF16)

    mm = _dot_nt if w_transposed else _dot
    o_ref[...] = mm(h_ref[...], wb_scr[...]).astype(o_ref.dtype)


def _inproj(h2, w_all, layer_idx, nout, tm, tn, w_transposed=False):
    n, d = h2.shape
    if w_transposed:
        w_spec = pl.BlockSpec((1, tn, d), lambda j, i: (layer_idx, j, 0))
        w_tile = (tn, d)
    else:
        w_spec = pl.BlockSpec((1, d, tn), lambda j, i: (layer_idx, 0, j))
        w_tile = (d, tn)
    return pl.pallas_call(
        functools.partial(_inproj_kernel, w_transposed=w_transposed),
        grid=(nout // tn, n // tm),
        in_specs=[pl.BlockSpec((tm, d), lambda j, i: (i, 0)), w_spec],
        out_specs=pl.BlockSpec((tm, tn), lambda j, i: (i, j)),
        out_shape=jax.ShapeDtypeStruct((n, nout), BF16),
        scratch_shapes=[pltpu.VMEM(w_tile, BF16)],
        compiler_params=pltpu.CompilerParams(
            dimension_semantics=("arbitrary", "arbitrary"), vmem_limit_bytes=VMEM_LIMIT),
        name="inproj_t" if w_transposed else "inproj",
    )(h2, w_all)


def _outproj_kernel(*refs, n_y, final, has_tail):
    y_refs = refs[:n_y]
    w_ref, x_ref, m_ref, nw_ref = refs[n_y:n_y + 4]
    rest = list(refs[n_y + 4:])
    wb_scr = rest.pop()
    if not final:
        mn_ref = rest.pop(0)
        wt_ref = rest.pop(0) if has_tail else None
    o_ref = rest.pop(0)

    @pl.when(pl.program_id(0) == 0)
    def _():
        wb_scr[...] = w_ref[0].astype(BF16)

    gate = m_ref[0][2:3, :]
    rows = x_ref.shape[0]
    chunk = rows // OUT_ROW_CHUNKS
    for r0 in range(0, rows, chunk):
        rs = slice(r0, r0 + chunk)
        acc = None
        row = 0
        for y_ref in y_refs:
            kw = y_ref.shape[1]
            part = _dot(y_ref[rs, :], wb_scr[row:row + kw, :])
            acc = part if acc is None else acc + part
            row += kw
        xn = x_ref[rs, :] + gate * acc
        if final:
            ms = jnp.mean(xn * xn, axis=-1, keepdims=True)
            o_ref[rs, :] = xn * lax.rsqrt(ms + RMS_EPS) * nw_ref[...]
        else:
            o_ref[rs, :] = xn
            h = _modulated_norm(xn, nw_ref[...], mn_ref[0])
            rest[0][rs, :] = h
            if has_tail:
                rest[1][rs, :] = _tail_proj(h, wt_ref)


def _outproj(ys, w_all, layer_idx, x2, mod3, next_norm_w, next_mod3, next_tail, seq, tm):
    n, d = x2.shape
    per_b = seq // tm
    final = next_mod3 is None
    has_tail = next_tail is not None
    row_spec = pl.BlockSpec((tm, d), lambda i: (i, 0))
    mod_spec = pl.BlockSpec((1, 3, d), lambda i: (i // per_b, 0, 0))
    wshape = (1,) + w_all.shape[1:]
    in_specs = [pl.BlockSpec((tm, y.shape[1]), lambda i: (i, 0)) for y in ys]
    in_specs += [pl.BlockSpec(wshape, lambda i: (layer_idx, 0, 0), pipeline_mode=pl.Buffered(1)),
                 row_spec, mod_spec, pl.BlockSpec((1, d), lambda i: (0, 0))]
    args = list(ys) + [w_all, x2, mod3, next_norm_w.reshape(1, d)]
    if final:
        out_specs = row_spec
        out_shape = jax.ShapeDtypeStruct((n, d), F32)
    else:
        in_specs.append(mod_spec)
        args.append(next_mod3)
        out_specs = [row_spec, row_spec]
        out_shape = [jax.ShapeDtypeStruct((n, d), F32), jax.ShapeDtypeStruct((n, d), BF16)]
        if has_tail:
            in_specs.append(_tail_spec(next_tail[1], d))
            args.append(next_tail[0])
            out_specs.append(pl.BlockSpec((tm, LANES), lambda i: (i, 0)))
            out_shape.append(jax.ShapeDtypeStruct((n, LANES), F32))
    res = pl.pallas_call(
        functools.partial(_outproj_kernel, n_y=len(ys), final=final, has_tail=has_tail),
        grid=(n // tm,),
        in_specs=in_specs,
        out_specs=out_specs,
        out_shape=out_shape,
        scratch_shapes=[pltpu.VMEM(w_all.shape[1:], BF16)],
        compiler_params=pltpu.CompilerParams(
            dimension_semantics=("arbitrary",), vmem_limit_bytes=VMEM_LIMIT),
        name="outproj_final" if final else ("outproj_tail" if has_tail else "outproj"),
    )(*args)
    if final:
        return res, None, None
    return (res[0], res[1], res[2]) if has_tail else (res[0], res[1], None)


def _da_kernel(slope_ref, lam_ref, q_ref, k_ref, v_ref, z_ref, nw_ref, o_ref,
               m_scr, l_scr, acc_scr, *, tq, tk, lam_init):
    h = pl.program_id(1)
    qi = pl.program_id(2)
    slope = slope_ref[h]
    i0 = qi * tq

    q = q_ref[...].astype(F32) * (DA_QK_DIM ** -0.5 * LOG2E)
    lane = lax.broadcasted_iota(jnp.int32, q.shape, 1)
    qmaps = (jnp.where(lane < DA_QK_DIM, q, 0.0).astype(BF16), jnp.where(lane >= DA_QK_DIM, q, 0.0).astype(BF16))
    slope2 = slope * LOG2E

    m_scr[...] = jnp.full(m_scr.shape, NEG_BIG, F32)
    l_scr[...] = jnp.zeros(l_scr.shape, F32)
    acc_scr[...] = jnp.zeros(acc_scr.shape, F32)

    def tile(ks, nk, q0, masked):
        nq = tq - q0
        cols = ((q0, tq), (tq + q0, 2 * tq))
        k = k_ref[pl.ds(ks, nk), :]
        v = v_ref[pl.ds(ks, nk), :]
        qst = jnp.concatenate([qm[q0:] for qm in qmaps], axis=0)
        key_i = lax.broadcasted_iota(jnp.int32, (nk, LANES), 0)
        bias = slope2 * (key_i + (ks - i0)).astype(F32)
        s = _dot_nt(k, qst) + jnp.concatenate([bias] * (2 * nq // LANES), axis=1)
        if masked:
            kj = lax.broadcasted_iota(jnp.int32, (nk, 2 * nq), 0) + (ks - i0)
            qc = (lax.broadcasted_iota(jnp.int32, (nk, 2 * nq), 1) & (nq - 1)) + q0
            s = jnp.where(kj <= qc, s, NEG_BIG)
        gather = lambda ref: jnp.concatenate([ref[:, a:b] for a, b in cols], axis=1)
        m_prev = gather(m_scr)
        m_new = jnp.maximum(m_prev, jnp.max(s, axis=0, keepdims=True))
        alpha = jnp.exp2(m_prev - m_new)
        p = jnp.exp2(s - m_new)
        l_new = alpha * gather(l_scr) + jnp.sum(p, axis=0, keepdims=True)
        acc_new = alpha * gather(acc_scr) + _dot_tn(v, p.astype(BF16))
        for ci, (a, b) in enumerate(cols):
            m_scr[:, a:b] = m_new[:, ci * nq:(ci + 1) * nq]
            l_scr[:, a:b] = l_new[:, ci * nq:(ci + 1) * nq]
            acc_scr[:, a:b] = acc_new[:, ci * nq:(ci + 1) * nq]

    def body(ki, carry):
        tile(pl.multiple_of(ki * tk, tk), tk, 0, False)
        return carry

    lax.fori_loop(0, qi, body, 0)
    half = tk // 2
    tile(pl.multiple_of(i0, tk), half, 0, True)
    tile(pl.multiple_of(i0 + half, half), half, tq // 2, True)

    lv = lam_ref[...]
    lam = (jnp.exp(jnp.sum(lv[0:1] * lv[1:2], axis=-1, keepdims=True))
           - jnp.exp(jnp.sum(lv[2:3] * lv[3:4], axis=-1, keepdims=True)) + lam_init)
    acc = acc_scr[...] / l_scr[...]
    o = (acc[:, :tq] - lam * acc[:, tq:]).T
    ms = jnp.mean(o * o, axis=-1, keepdims=True)
    o = o * lax.rsqrt(ms + RMS_EPS) * nw_ref[...] * (1.0 - lam_init)
    z = z_ref[...].astype(F32)
    o_ref[...] = (o * _silu(z)).astype(o_ref.dtype)


def _diff_attention(proj, lamv, diff_norm, batch, seq, lam_init, tq=512):
    n = proj.shape[0]
    nq = seq // tq
    hq = DA_HEADS
    return pl.pallas_call(
        functools.partial(_da_kernel, tq=tq, tk=tq, lam_init=lam_init),
        grid=(batch, hq, nq),
        in_specs=[
            pl.BlockSpec(memory_space=pltpu.SMEM),
            pl.BlockSpec((4, DA_QK_DIM), lambda b, h, i: (0, 0)),
            pl.BlockSpec((tq, LANES), lambda b, h, i: (b * nq + i, h)),
            pl.BlockSpec((seq, LANES), lambda b, h, i: (b, hq + h)),
            pl.BlockSpec((seq, LANES), lambda b, h, i: (b, 2 * hq + h)),
            pl.BlockSpec((tq, LANES), lambda b, h, i: (b * nq + i, 3 * hq + h)),
            pl.BlockSpec((1, DA_V_DIM), lambda b, h, i: (0, 0)),
        ],
        out_specs=pl.BlockSpec((tq, LANES), lambda b, h, i: (b * nq + i, h)),
        out_shape=jax.ShapeDtypeStruct((n, DA_WIDTH), BF16),
        scratch_shapes=[
            pltpu.VMEM((1, 2 * tq), F32),
            pltpu.VMEM((1, 2 * tq), F32),
            pltpu.VMEM((DA_V_DIM, 2 * tq), F32),
        ],
        compiler_params=pltpu.CompilerParams(
            dimension_semantics=("arbitrary", "arbitrary", "arbitrary"), vmem_limit_bytes=VMEM_LIMIT),
        name="diff_attn",
    )(_alibi_slopes(hq), lamv, proj, proj, proj, proj, diff_norm.reshape(1, DA_V_DIM))


GD_HALO = 8
GD_ROWS = 2 * GD_CHUNK


def _bdot(a, b):
    return _dot(a.astype(BF16), b.astype(BF16))


def _split3(x):
    b1 = x.astype(BF16)
    r1 = x - b1.astype(F32)
    b2 = r1.astype(BF16)
    b3 = (r1 - b2.astype(F32)).astype(BF16)
    return b1, b2, b3


def _gdn_kernel(q_ref, k_ref, v_ref, z_ref, t_ref, cw_ref, alog_ref, dtb_ref, nw_ref, o_ref,
                xbuf, s_scr):
    cs = GD_CHUNK
    rows = GD_ROWS
    heads = range(GD_HEADS)
    step = pl.program_id(1)

    @pl.when(step == 0)
    def _():
        xbuf[0:GD_HALO, :] = jnp.zeros((GD_HALO, GD_QKV), F32)
        s_scr[...] = jnp.zeros(s_scr.shape, F32)

    @pl.when(step > 0)
    def _():
        xbuf[0:GD_HALO, :] = xbuf[rows:rows + GD_HALO, :]

    xbuf[GD_HALO:GD_HALO + rows, 0:GD_WIDTH] = q_ref[...].astype(F32)
    xbuf[GD_HALO:GD_HALO + rows, GD_WIDTH:2 * GD_WIDTH] = k_ref[...].astype(F32)
    xbuf[GD_HALO:GD_HALO + rows, 2 * GD_WIDTH:3 * GD_WIDTH] = v_ref[...].astype(F32)

    def conv_silu(col0):
        xs = xbuf[:, col0:col0 + LANES]
        acc = None
        for kk in range(GD_CONV):
            shift = GD_CONV - 1 - kk
            xk = xs if shift == 0 else pltpu.roll(xs, shift, 0)
            term = cw_ref[kk:kk + 1, col0:col0 + LANES] * xk[GD_HALO:]
            acc = term if acc is None else acc + term
        return _silu(acc)

    def l2n(x):
        return x * lax.rsqrt(jnp.sum(x * x, axis=-1, keepdims=True) + RMS_EPS)

    tail = t_ref[...]
    beta_all = jax.nn.sigmoid(tail)
    g_all = -jnp.exp(alog_ref[...]) * jax.nn.softplus(tail + dtb_ref[...])

    ri = lax.broadcasted_iota(jnp.int32, (rows, rows), 0)
    ci = lax.broadcasted_iota(jnp.int32, (rows, rows), 1)
    same = (ri // cs) == (ci // cs)
    tril = same & (ri >= ci)
    strict = same & (ri > ci)
    tril_b = jnp.where(tril, 1.0, 0.0).astype(BF16)
    triu_b = jnp.where(same & (ri <= ci), 1.0, 0.0).astype(BF16)
    eye_f = jnp.where(ri == ci, 1.0, 0.0)

    gcs = _dot(tril_b, jnp.concatenate(_split3(g_all), axis=1))
    gc_all = gcs[:, :LANES] + gcs[:, LANES:2 * LANES] + gcs[:, 2 * LANES:]
    gts = _dot(jnp.concatenate(_split3(g_all.T), axis=0), triu_b)
    gc_t = gts[:LANES] + gts[LANES:2 * LANES] + gts[2 * LANES:]

    qs, ks, vs, bcols, gcws, gcbs, egcs = [], [], [], [], [], [], []
    for h in heads:
        qs.append(l2n(conv_silu(h * GD_DK)) * (GD_DK ** -0.5))
        ks.append(l2n(conv_silu(GD_WIDTH + h * GD_DK)))
        vs.append(conv_silu(2 * GD_WIDTH + h * GD_DV))
        bcols.append(beta_all[:, h:h + 1])
        gcw = jnp.broadcast_to(gc_all[:, GD_HEADS + h:GD_HEADS + h + 1], (rows, rows))
        gcws.append(gcw)
        gcbs.append(gcw[:, :LANES])
        egcs.append(jnp.exp(gcw[:, :LANES]))

    kqs = [_dot_nt(jnp.concatenate([qs[h], ks[h]], axis=0).astype(BF16), ks[h].astype(BF16)) for h in heads]

    m_lows, qkms = [], []
    for h in heads:
        dmat = gcws[h] - gc_t[GD_HEADS + h:GD_HEADS + h + 1, :]
        decay = jnp.exp(jnp.where(tril, dmat, NEG_BIG))
        m_lows.append(jnp.where(strict, bcols[h] * kqs[h][rows:] * decay, 0.0))
        qkms.append(jnp.where(tril, kqs[h][:rows] * decay, 0.0))

    blk = lambda sz: (ri // sz) == (ci // sz)
    tinvs = [eye_f - jnp.where(blk(2), m_lows[h], 0.0) for h in heads]
    sz = 4
    while sz <= cs:
        join = blk(sz) & jnp.logical_not(blk(sz // 2))
        tcs = [_bdot(tinvs[h], jnp.where(join, m_lows[h], 0.0)) for h in heads]
        tinvs = [tinvs[h] - _bdot(tcs[h], tinvs[h]) for h in heads]
        sz *= 2

    sols = [_bdot(tinvs[h], jnp.concatenate([bcols[h] * vs[h], bcols[h] * ks[h] * egcs[h]], axis=1)) for h in heads]

    states = [s_scr[h] for h in heads]
    v_news = [[] for _ in heads]
    o_inters = [[] for _ in heads]
    for c in range(rows // cs):
        r0 = c * cs
        for h in heads:
            s = states[h]
            gcb_c = gcbs[h][r0:r0 + cs]
            gl = gcb_c[cs - 1:cs, :]
            v_new = sols[h][r0:r0 + cs, :GD_DV] - _bdot(sols[h][r0:r0 + cs, GD_DV:], s)
            o_inters[h].append(_bdot(qs[h][r0:r0 + cs] * egcs[h][r0:r0 + cs], s))
            kd = ks[h][r0:r0 + cs] * jnp.exp(gl - gcb_c)
            states[h] = s * jnp.exp(gl) + _dot_tn(kd.astype(BF16), v_new.astype(BF16))
            v_news[h].append(v_new)

    for h in heads:
        s_scr[h] = states[h]
        o = jnp.concatenate(o_inters[h], axis=0) + _bdot(qkms[h], jnp.concatenate(v_news[h], axis=0))
        ms = jnp.mean(o * o, axis=-1, keepdims=True)
        o = o * lax.rsqrt(ms + RMS_EPS) * nw_ref[...]
        z = z_ref[:, h * GD_DV:(h + 1) * GD_DV].astype(F32)
        o_ref[:, h * GD_DV:(h + 1) * GD_DV] = (o * _silu(z)).astype(o_ref.dtype)


def _gated_deltanet(proj, tail, conv_w, a_log, dt_bias, delta_norm, batch, seq):
    n = proj.shape[0]
    cs = GD_ROWS
    nc = seq // cs
    pad = jnp.zeros((GD_HEADS,), F32)
    alog_row = jnp.concatenate([pad, a_log.astype(F32), jnp.zeros((LANES - 2 * GD_HEADS,), F32)]).reshape(1, LANES)
    dtb_row = jnp.concatenate([pad, dt_bias.astype(F32), jnp.zeros((LANES - 2 * GD_HEADS,), F32)]).reshape(1, LANES)
    cb = (4 * DA_WIDTH) // GD_WIDTH
    return pl.pallas_call(
        _gdn_kernel,
        grid=(batch, nc),
        in_specs=[
            pl.BlockSpec((cs, GD_WIDTH), lambda b, c: (b * nc + c, cb)),
            pl.BlockSpec((cs, GD_WIDTH), lambda b, c: (b * nc + c, cb + 1)),
            pl.BlockSpec((cs, GD_WIDTH), lambda b, c: (b * nc + c, cb + 2)),
            pl.BlockSpec((cs, GD_WIDTH), lambda b, c: (b * nc + c, cb + 3)),
            pl.BlockSpec((cs, LANES), lambda b, c: (b * nc + c, 0)),
            pl.BlockSpec((GD_CONV, GD_QKV), lambda b, c: (0, 0)),
            pl.BlockSpec((1, LANES), lambda b, c: (0, 0)),
            pl.BlockSpec((1, LANES), lambda b, c: (0, 0)),
            pl.BlockSpec((1, GD_DV), lambda b, c: (0, 0)),
        ],
        out_specs=pl.BlockSpec((cs, GD_WIDTH), lambda b, c: (b * nc + c, 0)),
        out_shape=jax.ShapeDtypeStruct((n, GD_WIDTH), BF16),
        scratch_shapes=[
            pltpu.VMEM((GD_HALO + cs, GD_QKV), F32),
            pltpu.VMEM((GD_HEADS, GD_DK, GD_DV), F32),
        ],
        compiler_params=pltpu.CompilerParams(
            dimension_semantics=("arbitrary", "arbitrary"), vmem_limit_bytes=VMEM_LIMIT),
        name="gated_deltanet",
    )(proj, proj, proj, proj, tail, conv_w, alog_row, dtb_row, delta_norm.reshape(1, GD_DV))


def _swa_bias_table(nheads):
    w = SW_WINDOW
    qi = jnp.arange(w)[:, None]
    kj = jnp.arange(2 * w)[None, :]
    dist = qi + w - kj
    in_window = (dist >= 0) & (dist < w)
    bias = -_alibi_slopes(nheads)[:, None, None] * dist.astype(F32)[None] * LOG2E
    general = jnp.where(in_window[None], bias, NEG_BIG)
    first = jnp.where((in_window & (kj >= w))[None], bias, NEG_BIG)
    return jnp.stack([first, general])


def _swa_kernel(sink_ref, bias_ref, q_ref, kc_ref, kp_ref, vc_ref, vp_ref,
                z0_ref, z1_ref, z2_ref, z3_ref, o_ref):
    w = SW_WINDOW
    hd = SW_HEAD_DIM
    z_refs = (z0_ref, z1_ref, z2_ref, z3_ref)

    kband = jnp.concatenate([kp_ref[...], kc_ref[...]], axis=0).astype(F32)
    vband = jnp.concatenate([vp_ref[...], vc_ref[...]], axis=0).astype(F32)

    lane = lax.broadcasted_iota(jnp.int32, (2 * w, LANES), 1)
    lo = lane < hd
    slabs = range(SW_GROUP * hd // LANES)
    kvs = range(SW_KV_HEADS)
    halves = range(2)
    ones_b = jnp.ones((2 * w, LANES), BF16)

    q4s, kzs, vxs = [], [], []
    for hk in kvs:
        g0 = (hk // 2) * LANES
        kg = kband[:, g0:g0 + LANES]
        vg = vband[:, g0:g0 + LANES]
        kr = pltpu.roll(kg, hd, 1)
        vr = pltpu.roll(vg, hd, 1)
        if hk % 2 == 0:
            k_lo, k_hi, v_lo, v_hi = kg, kr, vg, vr
        else:
            k_lo, k_hi, v_lo, v_hi = kr, kg, vr, vg
        kzs.append((jnp.where(lo, k_lo, 0.0).astype(BF16), jnp.where(lo, 0.0, k_hi).astype(BF16)))
        vxs.append((jnp.concatenate([jnp.where(lo, v_lo, 0.0).astype(BF16), ones_b], axis=1),
                    jnp.concatenate([jnp.where(lo, 0.0, v_hi).astype(BF16), ones_b], axis=1)))
        q4 = jnp.concatenate([q_ref[:, (hk * len(slabs) + sl) * LANES:(hk * len(slabs) + sl + 1) * LANES]
                              for sl in slabs], axis=0)
        q4s.append((q4.astype(F32) * (hd ** -0.5 * LOG2E)).astype(BF16))

    def qk(hk):
        return [_dot_nt(q4s[hk], kzs[hk][half]) for half in halves]

    scores = {0: qk(0)}
    for hk in kvs:
        if hk + 1 < SW_KV_HEADS:
            scores[hk + 1] = qk(hk + 1)
        o = None
        for half in halves:
            ps, es = [], []
            for sl in slabs:
                head = hk * SW_GROUP + 2 * sl + half
                sink = sink_ref[head] * LOG2E
                s = scores[hk][half][sl * w:(sl + 1) * w] + bias_ref[0, head]
                m = jnp.maximum(jnp.max(s, axis=-1, keepdims=True), sink)
                ps.append(jnp.exp2(s - m).astype(BF16))
                es.append(jnp.exp2(sink - m))
            r = _dot(jnp.concatenate(ps, axis=0), vxs[hk][half])
            part = r[:, :LANES] / (r[:, LANES:] + jnp.concatenate(es, axis=0))
            o = part if o is None else o + part
        for sl in slabs:
            c0 = (hk * len(slabs) + sl) * LANES
            z = z_refs[hk][:, sl * LANES:(sl + 1) * LANES].astype(F32)
            o_ref[:, c0:c0 + LANES] = (o[sl * w:(sl + 1) * w] * _silu(z)).astype(o_ref.dtype)


def _sliding_window_attention(proj, sinks, batch, seq):
    n = proj.shape[0]
    w = SW_WINDOW
    nb = seq // w
    kcb = SW_WIDTH // SW_KV_WIDTH
    zw = SW_WIDTH // SW_KV_HEADS
    zcb = (SW_WIDTH + 2 * SW_KV_WIDTH) // zw
    cur = lambda b, i: b * nb + i
    prev = lambda b, i: b * nb + jnp.maximum(i - 1, 0)
    in_specs = [
        pl.BlockSpec(memory_space=pltpu.SMEM),
        pl.BlockSpec((1, SW_HEADS, w, 2 * w), lambda b, i: (jnp.minimum(i, 1), 0, 0, 0)),
        pl.BlockSpec((w, SW_WIDTH), lambda b, i: (cur(b, i), 0)),
        pl.BlockSpec((w, SW_KV_WIDTH), lambda b, i: (cur(b, i), kcb)),
        pl.BlockSpec((w, SW_KV_WIDTH), lambda b, i: (prev(b, i), kcb)),
        pl.BlockSpec((w, SW_KV_WIDTH), lambda b, i: (cur(b, i), kcb + 1)),
        pl.BlockSpec((w, SW_KV_WIDTH), lambda b, i: (prev(b, i), kcb + 1)),
    ]
    for hk in range(SW_KV_HEADS):
        in_specs.append(pl.BlockSpec((w, zw), lambda b, i, hk=hk: (cur(b, i), zcb + hk)))
    return pl.pallas_call(
        _swa_kernel,
        grid=(batch, nb),
        in_specs=in_specs,
        out_specs=pl.BlockSpec((w, SW_WIDTH), lambda b, i: (cur(b, i), 0)),
        out_shape=jax.ShapeDtypeStruct((n, SW_WIDTH), BF16),
        compiler_params=pltpu.CompilerParams(
            dimension_semantics=("arbitrary", "arbitrary"), vmem_limit_bytes=VMEM_LIMIT),
        name="swa",
    )(sinks.astype(F32), _swa_bias_table(SW_HEADS), *([proj] * 9))


def kernel(x, c, norm_w, ada_w, ada_b, ev_w_in, ev_w_out, ev_lam_q1, ev_lam_k1, ev_lam_q2, ev_lam_k2, ev_diff_norm, ev_conv_w, ev_A_log, ev_dt_bias, ev_delta_norm, od_w_in, od_w_out, od_sinks, final_norm_w):
    batch, seq, d = x.shape
    depth = ada_w.shape[0]
    n = batch * seq
    x2 = x.reshape(n, d)

    mod = _ada_mod(c, ada_w, ada_b)
    tm_out = min(512, seq)
    tm_in = min(1024, seq)
    mod3s = [mod[layer].reshape(batch, 3, d) for layer in range(depth)]

    ev_w_in_t = jnp.swapaxes(ev_w_in, 1, 2)

    h2, tail = _prenorm(x2, norm_w[0], mod3s[0], ev_w_in_t, 0, seq, tm_out)
    for layer in range(depth):
        i = layer // 2
        last = layer == depth - 1
        next_norm_w = final_norm_w if last else norm_w[layer + 1]
        next_mod3 = None if last else mod3s[layer + 1]
        next_tail = (ev_w_in_t, (layer + 1) // 2) if (not last and layer % 2 == 1) else None
        if layer % 2 == 0:
            lam_init = 0.8 - 0.6 * math.exp(-0.3 * layer)
            proj = _inproj(h2, ev_w_in_t, i, EVEN_MAIN, tm_in, 1024, w_transposed=True)
            lamv = jnp.stack([ev_lam_q1[i], ev_lam_k1[i], ev_lam_q2[i], ev_lam_k2[i]]).astype(F32)
            ya = _diff_attention(proj, lamv, ev_diff_norm[i], batch, seq, lam_init, tq=min(512, seq))
            yb = _gated_deltanet(proj, tail, ev_conv_w[i], ev_A_log[i], ev_dt_bias[i], ev_delta_norm[i], batch, seq)
            ys, w_out = [ya, yb], ev_w_out
        else:
            proj = _inproj(h2, od_w_in, i, ODD_IN, tm_in, 1536)
            ys, w_out = [_sliding_window_attention(proj, od_sinks[i], batch, seq)], od_w_out
        x2, h2, tail = _outproj(ys, w_out, i, x2, mod3s[layer], next_norm_w, next_mod3, next_tail, seq, tm_out)
    return x2.reshape(batch, seq, d)
```

```python
import functools
import math

import jax
import jax.numpy as jnp
from jax import lax
from jax.experimental import pallas as pl
from jax.experimental.pallas import tpu as pltpu

F32 = jnp.float32
BF16 = jnp.bfloat16

D_MODEL = 2048
DEPTH = 4
RMS_EPS = 1e-6

DA_HEADS = 8
DA_QK_DIM = 64
DA_V_DIM = 128
DA_WIDTH = DA_HEADS * DA_V_DIM
GD_HEADS = 8
GD_DK = 128
GD_DV = 128
GD_WIDTH = GD_HEADS * GD_DV
GD_CONV = 4
GD_CHUNK = 64
GD_QKV = 3 * GD_WIDTH

EVEN_MAIN = 4 * DA_WIDTH + GD_QKV + GD_WIDTH
EVEN_IN = EVEN_MAIN + 2 * GD_HEADS

SW_HEADS = 32
SW_KV_HEADS = 4
SW_GROUP = SW_HEADS // SW_KV_HEADS
SW_HEAD_DIM = 64
SW_WIDTH = SW_HEADS * SW_HEAD_DIM
SW_KV_WIDTH = SW_KV_HEADS * SW_HEAD_DIM
SW_WINDOW = 128
ODD_IN = 2 * SW_WIDTH + 2 * SW_KV_WIDTH

LANES = 128
NEG_BIG = -1e30
LOG2E = math.log2(math.e)
OUT_ROW_CHUNKS = 2
VMEM_LIMIT = 56 * 1024 * 1024

HIGHEST = lax.Precision.HIGHEST


def _dot(a, b, precision=None):
    return jnp.dot(a, b, preferred_element_type=F32, precision=precision)


def _dot_nt(a, b, precision=None):
    return lax.dot_general(a, b, (((1,), (1,)), ((), ())), preferred_element_type=F32, precision=precision)


def _dot_tn(a, b, precision=None):
    return lax.dot_general(a, b, (((0,), (0,)), ((), ())), preferred_element_type=F32, precision=precision)


def _silu(x):
    return x * jax.nn.sigmoid(x)


def _alibi_slopes(n):
    return 2.0 ** (-8.0 * jnp.arange(1, n + 1, dtype=F32) / n)


def _ada_kernel(c_ref, w_ref, b_ref, o_ref):
    c = c_ref[...]
    cond = _silu(c).astype(BF16)
    w = w_ref[0].astype(BF16)
    o_ref[0] = _dot(cond, w) + b_ref[0]


def _ada_mod(c, ada_w, ada_b, tn=512):
    depth, d, n3 = ada_w.shape
    b = c.shape[0]
    return pl.pallas_call(
        _ada_kernel,
        grid=(depth, n3 // tn),
        in_specs=[
            pl.BlockSpec((b, d), lambda l, j: (0, 0)),
            pl.BlockSpec((1, d, tn), lambda l, j: (l, 0, j)),
            pl.BlockSpec((1, 1, tn), lambda l, j: (l, 0, j)),
        ],
        out_specs=pl.BlockSpec((1, b, tn), lambda l, j: (l, 0, j)),
        out_shape=jax.ShapeDtypeStruct((depth, b, n3), F32),
        compiler_params=pltpu.CompilerParams(
            dimension_semantics=("arbitrary", "arbitrary"), vmem_limit_bytes=VMEM_LIMIT),
        name="ada_mod",
    )(c, ada_w, ada_b.reshape(depth, 1, n3))


def _modulated_norm(x, nw, m):
    ms = jnp.mean(x * x, axis=-1, keepdims=True)
    y = x * lax.rsqrt(ms + RMS_EPS) * nw
    return (y * (1.0 + m[1:2, :]) + m[0:1, :]).astype(BF16)


def _tail_proj(h, wt_ref):
    w = wt_ref[0]
    row = lax.broadcasted_iota(jnp.int32, w.shape, 0)
    return _dot_nt(h, jnp.where(row < 2 * GD_HEADS, w, 0.0).astype(BF16))


def _tail_spec(layer_idx, d):
    return pl.BlockSpec((1, LANES, d), lambda i: (layer_idx, EVEN_MAIN // LANES, 0))


def _prenorm_kernel(x_ref, nw_ref, m_ref, wt_ref, h_ref, t_ref):
    h = _modulated_norm(x_ref[...], nw_ref[...], m_ref[0])
    h_ref[...] = h
    t_ref[...] = _tail_proj(h, wt_ref)


def _prenorm(x2, norm_w, mod3, ev_w_in, layer_idx, seq, tm):
    n, d = x2.shape
    per_b = seq // tm
    return pl.pallas_call(
        _prenorm_kernel,
        grid=(n // tm,),
        in_specs=[
            pl.BlockSpec((tm, d), lambda i: (i, 0)),
            pl.BlockSpec((1, d), lambda i: (0, 0)),
            pl.BlockSpec((1, 3, d), lambda i: (i // per_b, 0, 0)),
            _tail_spec(layer_idx, d),
        ],
        out_specs=[pl.BlockSpec((tm, d), lambda i: (i, 0)), pl.BlockSpec((tm, LANES), lambda i: (i, 0))],
        out_shape=[jax.ShapeDtypeStruct((n, d), BF16), jax.ShapeDtypeStruct((n, LANES), F32)],
        compiler_params=pltpu.CompilerParams(
            dimension_semantics=("arbitrary",), vmem_limit_bytes=VMEM_LIMIT),
        name="prenorm",
    )(x2, norm_w.reshape(1, d), mod3, ev_w_in)


def _inproj_kernel(h_ref, w_ref, o_ref, wb_scr, *, w_transposed):
    @pl.when(pl.program_id(1) == 0)
    def _():
        wb_scr[...] = w_ref[0].astype(BF16)

    mm = _dot_nt if w_transposed else _dot
    o_ref[...] = mm(h_ref[...], wb_scr[...]).astype(o_ref.dtype)


def _inproj(h2, w_all, layer_idx, nout, tm, tn, w_transposed=False):
    n, d = h2.shape
    if w_transposed:
        w_spec = pl.BlockSpec((1, tn, d), lambda j, i: (layer_idx, j, 0))
        w_tile = (tn, d)
    else:
        w_spec = pl.BlockSpec((1, d, tn), lambda j, i: (layer_idx, 0, j))
        w_tile = (d, tn)
    return pl.pallas_call(
        functools.partial(_inproj_kernel, w_transposed=w_transposed),
        grid=(nout // tn, n // tm),
        in_specs=[pl.BlockSpec((tm, d), lambda j, i: (i, 0)), w_spec],
        out_specs=pl.BlockSpec((tm, tn), lambda j, i: (i, j)),
        out_shape=jax.ShapeDtypeStruct((n, nout), BF16),
        scratch_shapes=[pltpu.VMEM(w_tile, BF16)],
        compiler_params=pltpu.CompilerParams(
            dimension_semantics=("arbitrary", "arbitrary"), vmem_limit_bytes=VMEM_LIMIT),
        name="inproj_t" if w_transposed else "inproj",
    )(h2, w_all)


def _outproj_kernel(*refs, n_y, final, has_tail):
    y_refs = refs[:n_y]
    w_ref, x_ref, m_ref, nw_ref = refs[n_y:n_y + 4]
    rest = list(refs[n_y + 4:])
    wb_scr = rest.pop()
    if not final:
        mn_ref = rest.pop(0)
        wt_ref = rest.pop(0) if has_tail else None
    o_ref = rest.pop(0)

    @pl.when(pl.program_id(0) == 0)
    def _():
        wb_scr[...] = w_ref[0].astype(BF16)

    gate = m_ref[0][2:3, :]
    rows = x_ref.shape[0]
    chunk = rows // OUT_ROW_CHUNKS
    for r0 in range(0, rows, chunk):
        rs = slice(r0, r0 + chunk)
        acc = None
        row = 0
        for y_ref in y_refs:
            kw = y_ref.shape[1]
            part = _dot(y_ref[rs, :], wb_scr[row:row + kw, :])
            acc = part if acc is None else acc + part
            row += kw
        xn = x_ref[rs, :] + gate * acc
        if final:
            ms = jnp.mean(xn * xn, axis=-1, keepdims=True)
            o_ref[rs, :] = xn * lax.rsqrt(ms + RMS_EPS) * nw_ref[...]
        else:
            o_ref[rs, :] = xn
            h = _modulated_norm(xn, nw_ref[...], mn_ref[0])
            rest[0][rs, :] = h
            if has_tail:
                rest[1][rs, :] = _tail_proj(h, wt_ref)


def _outproj(ys, w_all, layer_idx, x2, mod3, next_norm_w, next_mod3, next_tail, seq, tm):
    n, d = x2.shape
    per_b = seq // tm
    final = next_mod3 is None
    has_tail = next_tail is not None
    row_spec = pl.BlockSpec((tm, d), lambda i: (i, 0))
    mod_spec = pl.BlockSpec((1, 3, d), lambda i: (i // per_b, 0, 0))
    wshape = (1,) + w_all.shape[1:]
    in_specs = [pl.BlockSpec((tm, y.shape[1]), lambda i: (i, 0)) for y in ys]
    in_specs += [pl.BlockSpec(wshape, lambda i: (layer_idx, 0, 0), pipeline_mode=pl.Buffered(1)),
                 row_spec, mod_spec, pl.BlockSpec((1, d), lambda i: (0, 0))]
    args = list(ys) + [w_all, x2, mod3, next_norm_w.reshape(1, d)]
    if final:
        out_specs = row_spec
        out_shape = jax.ShapeDtypeStruct((n, d), F32)
    else:
        in_specs.append(mod_spec)
        args.append(next_mod3)
        out_specs = [row_spec, row_spec]
        out_shape = [jax.ShapeDtypeStruct((n, d), F32), jax.ShapeDtypeStruct((n, d), BF16)]
        if has_tail:
            in_specs.append(_tail_spec(next_tail[1], d))
            args.append(next_tail[0])
            out_specs.append(pl.BlockSpec((tm, LANES), lambda i: (i, 0)))
            out_shape.append(jax.ShapeDtypeStruct((n, LANES), F32))
    res = pl.pallas_call(
        functools.partial(_outproj_kernel, n_y=len(ys), final=final, has_tail=has_tail),
        grid=(n // tm,),
        in_specs=in_specs,
        out_specs=out_specs,
        out_shape=out_shape,
        scratch_shapes=[pltpu.VMEM(w_all.shape[1:], BF16)],
        compiler_params=pltpu.CompilerParams(
            dimension_semantics=("arbitrary",), vmem_limit_bytes=VMEM_LIMIT),
        name="outproj_final" if final else ("outproj_tail" if has_tail else "outproj"),
    )(*args)
    if final:
        return res, None, None
    return (res[0], res[1], res[2]) if has_tail else (res[0], res[1], None)


def _da_kernel(slope_ref, lam_ref, q_ref, k_ref, v_ref, z_ref, nw_ref, o_ref, *, tq, tk, nq_tiles, lam_init):
    slope2 = slope_ref[pl.program_id(1)] * LOG2E
    qi = pl.program_id(2)
    for nfull in range(nq_tiles):
        pl.when(qi == nfull)(functools.partial(
            _da_tile_program, nfull, slope2, lam_ref, q_ref, k_ref, v_ref, z_ref, nw_ref, o_ref,
            tq=tq, tk=tk, lam_init=lam_init))


def _da_tile_program(nfull, slope2, lam_ref, q_ref, k_ref, v_ref, z_ref, nw_ref, o_ref, *, tq, tk, lam_init):
    i0 = nfull * tq
    q = q_ref[...].astype(F32) * (DA_QK_DIM ** -0.5 * LOG2E)
    lane = lax.broadcasted_iota(jnp.int32, q.shape, 1)
    qmaps = (jnp.where(lane < DA_QK_DIM, q, 0.0).astype(BF16), jnp.where(lane >= DA_QK_DIM, q, 0.0).astype(BF16))

    def scores(ks, nk, q0):
        qst = jnp.concatenate([qm[q0:] for qm in qmaps], axis=0)
        return _dot_nt(k_ref[ks:ks + nk, :], qst)

    def update(s, ks, nk, q0, masked, m_all, l_all, acc_all):
        nq = tq - q0
        v = v_ref[ks:ks + nk, :]
        key_i = lax.broadcasted_iota(jnp.int32, (nk, LANES), 0)
        bias = slope2 * (key_i + (ks - i0)).astype(F32)
        s = s + jnp.concatenate([bias] * (2 * nq // LANES), axis=1)
        if masked:
            kj = lax.broadcasted_iota(jnp.int32, (nk, 2 * nq), 0) + (ks - i0)
            qc = (lax.broadcasted_iota(jnp.int32, (nk, 2 * nq), 1) & (nq - 1)) + q0
            s = jnp.where(kj <= qc, s, NEG_BIG)
        gather = lambda x: jnp.concatenate([x[:, q0:tq], x[:, tq + q0:]], axis=1)
        scatter = lambda x, new: jnp.concatenate([x[:, :q0], new[:, :nq], x[:, tq:tq + q0], new[:, nq:]], axis=1)
        m_prev = gather(m_all)
        m_new = jnp.maximum(m_prev, jnp.max(s, axis=0, keepdims=True))
        alpha = jnp.exp2(m_prev - m_new)
        p = jnp.exp2(s - m_new)
        l_new = alpha * gather(l_all) + jnp.sum(p, axis=0, keepdims=True)
        acc_new = alpha * gather(acc_all) + _dot_tn(v, p.astype(BF16))
        if q0 == 0:
            return m_new, l_new, acc_new
        return scatter(m_all, m_new), scatter(l_all, l_new), scatter(acc_all, acc_new)

    half = tk // 2
    tiles = [(t * tk, tk, 0, False) for t in range(nfull)]
    tiles += [(i0, half, 0, True), (i0 + half, half, tq // 2, True)]
    m = jnp.full((1, 2 * tq), NEG_BIG, F32)
    l = jnp.zeros((1, 2 * tq), F32)
    acc = jnp.zeros((DA_V_DIM, 2 * tq), F32)
    s_next = scores(*tiles[0][:3])
    for t, (ks, nk, q0, masked) in enumerate(tiles):
        s_cur = s_next
        if t + 1 < len(tiles):
            s_next = scores(*tiles[t + 1][:3])
        m, l, acc = update(s_cur, ks, nk, q0, masked, m, l, acc)

    lv = lam_ref[...]
    lam = (jnp.exp(jnp.sum(lv[0:1] * lv[1:2], axis=-1, keepdims=True))
           - jnp.exp(jnp.sum(lv[2:3] * lv[3:4], axis=-1, keepdims=True)) + lam_init)
    acc = acc / l
    o = (acc[:, :tq] - lam * acc[:, tq:]).T
    ms = jnp.mean(o * o, axis=-1, keepdims=True)
    o = o * lax.rsqrt(ms + RMS_EPS) * nw_ref[...] * (1.0 - lam_init)
    z = z_ref[...].astype(F32)
    o_ref[...] = (o * _silu(z)).astype(o_ref.dtype)


def _diff_attention(proj, lamv, diff_norm, batch, seq, lam_init, tq=512):
    n = proj.shape[0]
    nq = seq // tq
    hq = DA_HEADS
    return pl.pallas_call(
        functools.partial(_da_kernel, tq=tq, tk=tq, nq_tiles=nq, lam_init=lam_init),
        grid=(batch, hq, nq),
        in_specs=[
            pl.BlockSpec(memory_space=pltpu.SMEM),
            pl.BlockSpec((4, DA_QK_DIM), lambda b, h, i: (0, 0)),
            pl.BlockSpec((tq, LANES), lambda b, h, i: (b * nq + i, h)),
            pl.BlockSpec((seq, LANES), lambda b, h, i: (b, hq + h)),
            pl.BlockSpec((seq, LANES), lambda b, h, i: (b, 2 * hq + h)),
            pl.BlockSpec((tq, LANES), lambda b, h, i: (b * nq + i, 3 * hq + h)),
            pl.BlockSpec((1, DA_V_DIM), lambda b, h, i: (0, 0)),
        ],
        out_specs=pl.BlockSpec((tq, LANES), lambda b, h, i: (b * nq + i, h)),
        out_shape=jax.ShapeDtypeStruct((n, DA_WIDTH), BF16),
        compiler_params=pltpu.CompilerParams(
            dimension_semantics=("arbitrary", "arbitrary", "arbitrary"), vmem_limit_bytes=VMEM_LIMIT),
        name="diff_attn",
    )(_alibi_slopes(hq), lamv, proj, proj, proj, proj, diff_norm.reshape(1, DA_V_DIM))


GD_HALO = 8
GD_ROWS = 2 * GD_CHUNK


def _bdot(a, b):
    return _dot(a.astype(BF16), b.astype(BF16))


def _split3(x):
    b1 = x.astype(BF16)
    r1 = x - b1.astype(F32)
    b2 = r1.astype(BF16)
    b3 = (r1 - b2.astype(F32)).astype(BF16)
    return b1, b2, b3


def _gdn_kernel(q_ref, k_ref, v_ref, z_ref, t_ref, cw_ref, alog_ref, dtb_ref, nw_ref, o_ref,
                xbuf, s_scr):
    cs = GD_CHUNK
    rows = GD_ROWS
    heads = range(GD_HEADS)
    step = pl.program_id(1)

    @pl.when(step == 0)
    def _():
        xbuf[0:GD_HALO, :] = jnp.zeros((GD_HALO, GD_QKV), F32)
        s_scr[...] = jnp.zeros(s_scr.shape, F32)

    @pl.when(step > 0)
    def _():
        xbuf[0:GD_HALO, :] = xbuf[rows:rows + GD_HALO, :]

    xbuf[GD_HALO:GD_HALO + rows, 0:GD_WIDTH] = q_ref[...].astype(F32)
    xbuf[GD_HALO:GD_HALO + rows, GD_WIDTH:2 * GD_WIDTH] = k_ref[...].astype(F32)
    xbuf[GD_HALO:GD_HALO + rows, 2 * GD_WIDTH:3 * GD_WIDTH] = v_ref[...].astype(F32)

    def conv_silu(col0):
        xs = xbuf[:, col0:col0 + LANES]
        acc = None
        for kk in range(GD_CONV):
            shift = GD_CONV - 1 - kk
            xk = xs if shift == 0 else pltpu.roll(xs, shift, 0)
            term = cw_ref[kk:kk + 1, col0:col0 + LANES] * xk[GD_HALO:]
            acc = term if acc is None else acc + term
        return _silu(acc)

    def l2n(x):
        return x * lax.rsqrt(jnp.sum(x * x, axis=-1, keepdims=True) + RMS_EPS)

    tail = t_ref[...]
    beta_all = jax.nn.sigmoid(tail)
    g_all = -jnp.exp(alog_ref[...]) * jax.nn.softplus(tail + dtb_ref[...])

    ri = lax.broadcasted_iota(jnp.int32, (rows, rows), 0)
    ci = lax.broadcasted_iota(jnp.int32, (rows, rows), 1)
    same = (ri // cs) == (ci // cs)
    tril = same & (ri >= ci)
    strict = same & (ri > ci)
    tril_b = jnp.where(tril, 1.0, 0.0).astype(BF16)
    triu_b = jnp.where(same & (ri <= ci), 1.0, 0.0).astype(BF16)
    eye_f = jnp.where(ri == ci, 1.0, 0.0)

    gcs = _dot(tril_b, jnp.concatenate(_split3(g_all), axis=1))
    gc_all = gcs[:, :LANES] + gcs[:, LANES:2 * LANES] + gcs[:, 2 * LANES:]
    gts = _dot(jnp.concatenate(_split3(g_all.T), axis=0), triu_b)
    gc_t = gts[:LANES] + gts[LANES:2 * LANES] + gts[2 * LANES:]

    qs, ks, vs, bcols, gcws, gcbs, egcs = [], [], [], [], [], [], []
    for h in heads:
        qs.append(l2n(conv_silu(h * GD_DK)) * (GD_DK ** -0.5))
        ks.append(l2n(conv_silu(GD_WIDTH + h * GD_DK)))
        vs.append(conv_silu(2 * GD_WIDTH + h * GD_DV))
        bcols.append(beta_all[:, h:h + 1])
        gcw = jnp.broadcast_to(gc_all[:, GD_HEADS + h:GD_HEADS + h + 1], (rows, rows))
        gcws.append(gcw)
        gcbs.append(gcw[:, :LANES])
        egcs.append(jnp.exp(gcw[:, :LANES]))

    kqs = [_dot_nt(jnp.concatenate([qs[h], ks[h]], axis=0).astype(BF16), ks[h].astype(BF16)) for h in heads]

    m_lows, qkms = [], []
    for h in heads:
        dmat = gcws[h] - gc_t[GD_HEADS + h:GD_HEADS + h + 1, :]
        decay = jnp.exp(jnp.where(tril, dmat, NEG_BIG))
        m_lows.append(jnp.where(strict, bcols[h] * kqs[h][rows:] * decay, 0.0))
        qkms.append(jnp.where(tril, kqs[h][:rows] * decay, 0.0))

    blk = lambda sz: (ri // sz) == (ci // sz)
    tinvs = [eye_f - jnp.where(blk(2), m_lows[h], 0.0) for h in heads]
    sz = 4
    while sz <= cs:
        join = blk(sz) & jnp.logical_not(blk(sz // 2))
        tcs = [_bdot(tinvs[h], jnp.where(join, m_lows[h], 0.0)) for h in heads]
        tinvs = [tinvs[h] - _bdot(tcs[h], tinvs[h]) for h in heads]
        sz *= 2

    sols = [_bdot(tinvs[h], jnp.concatenate([bcols[h] * vs[h], bcols[h] * ks[h] * egcs[h]], axis=1)) for h in heads]

    states = [s_scr[h] for h in heads]
    v_news = [[] for _ in heads]
    o_inters = [[] for _ in heads]
    for c in range(rows // cs):
        r0 = c * cs
        for h in heads:
            s = states[h]
            gcb_c = gcbs[h][r0:r0 + cs]
            gl = gcb_c[cs - 1:cs, :]
            v_new = sols[h][r0:r0 + cs, :GD_DV] - _bdot(sols[h][r0:r0 + cs, GD_DV:], s)
            o_inters[h].append(_bdot(qs[h][r0:r0 + cs] * egcs[h][r0:r0 + cs], s))
            kd = ks[h][r0:r0 + cs] * jnp.exp(gl - gcb_c)
            states[h] = s * jnp.exp(gl) + _dot_tn(kd.astype(BF16), v_new.astype(BF16))
            v_news[h].append(v_new)

    for h in heads:
        s_scr[h] = states[h]
        o = jnp.concatenate(o_inters[h], axis=0) + _bdot(qkms[h], jnp.concatenate(v_news[h], axis=0))
        ms = jnp.mean(o * o, axis=-1, keepdims=True)
        o = o * lax.rsqrt(ms + RMS_EPS) * nw_ref[...]
        z = z_ref[:, h * GD_DV:(h + 1) * GD_DV].astype(F32)
        o_ref[:, h * GD_DV:(h + 1) * GD_DV] = (o * _silu(z)).astype(o_ref.dtype)


def _gated_deltanet(proj, tail, conv_w, a_log, dt_bias, delta_norm, batch, seq):
    n = proj.shape[0]
    cs = GD_ROWS
    nc = seq // cs
    pad = jnp.zeros((GD_HEADS,), F32)
    alog_row = jnp.concatenate([pad, a_log.astype(F32), jnp.zeros((LANES - 2 * GD_HEADS,), F32)]).reshape(1, LANES)
    dtb_row = jnp.concatenate([pad, dt_bias.astype(F32), jnp.zeros((LANES - 2 * GD_HEADS,), F32)]).reshape(1, LANES)
    cb = (4 * DA_WIDTH) // GD_WIDTH
    return pl.pallas_call(
        _gdn_kernel,
        grid=(batch, nc),
        in_specs=[
            pl.BlockSpec((cs, GD_WIDTH), lambda b, c: (b * nc + c, cb)),
            pl.BlockSpec((cs, GD_WIDTH), lambda b, c: (b * nc + c, cb + 1)),
            pl.BlockSpec((cs, GD_WIDTH), lambda b, c: (b * nc + c, cb + 2)),
            pl.BlockSpec((cs, GD_WIDTH), lambda b, c: (b * nc + c, cb + 3)),
            pl.BlockSpec((cs, LANES), lambda b, c: (b * nc + c, 0)),
            pl.BlockSpec((GD_CONV, GD_QKV), lambda b, c: (0, 0)),
            pl.BlockSpec((1, LANES), lambda b, c: (0, 0)),
            pl.BlockSpec((1, LANES), lambda b, c: (0, 0)),
            pl.BlockSpec((1, GD_DV), lambda b, c: (0, 0)),
        ],
        out_specs=pl.BlockSpec((cs, GD_WIDTH), lambda b, c: (b * nc + c, 0)),
        out_shape=jax.ShapeDtypeStruct((n, GD_WIDTH), BF16),
        scratch_shapes=[
            pltpu.VMEM((GD_HALO + cs, GD_QKV), F32),
            pltpu.VMEM((GD_HEADS, GD_DK, GD_DV), F32),
        ],
        compiler_params=pltpu.CompilerParams(
            dimension_semantics=("arbitrary", "arbitrary"), vmem_limit_bytes=VMEM_LIMIT),
        name="gated_deltanet",
    )(proj, proj, proj, proj, tail, conv_w, alog_row, dtb_row, delta_norm.reshape(1, GD_DV))


def _swa_bias_table(nheads):
    w = SW_WINDOW
    qi = jnp.arange(w)[:, None]
    kj = jnp.arange(2 * w)[None, :]
    dist = qi + w - kj
    in_window = (dist >= 0) & (dist < w)
    bias = -_alibi_slopes(nheads)[:, None, None] * dist.astype(F32)[None] * LOG2E
    general = jnp.where(in_window[None], bias, NEG_BIG)
    first = jnp.where((in_window & (kj >= w))[None], bias, NEG_BIG)
    return jnp.stack([first, general])


def _swa_kernel(sink_ref, bias_ref, q_ref, kc_ref, kp_ref, vc_ref, vp_ref,
                z0_ref, z1_ref, z2_ref, z3_ref, o_ref):
    w = SW_WINDOW
    hd = SW_HEAD_DIM
    z_refs = (z0_ref, z1_ref, z2_ref, z3_ref)

    kband = jnp.concatenate([kp_ref[...], kc_ref[...]], axis=0).astype(F32)
    vband = jnp.concatenate([vp_ref[...], vc_ref[...]], axis=0).astype(F32)

    lane = lax.broadcasted_iota(jnp.int32, (2 * w, LANES), 1)
    lo = lane < hd
    slabs = range(SW_GROUP * hd // LANES)
    kvs = range(SW_KV_HEADS)
    halves = range(2)
    ones_b = jnp.ones((2 * w, LANES), BF16)

    q4s, kzs, vxs = [], [], []
    for hk in kvs:
        g0 = (hk // 2) * LANES
        kg = kband[:, g0:g0 + LANES]
        vg = vband[:, g0:g0 + LANES]
        kr = pltpu.roll(kg, hd, 1)
        vr = pltpu.roll(vg, hd, 1)
        if hk % 2 == 0:
            k_lo, k_hi, v_lo, v_hi = kg, kr, vg, vr
        else:
            k_lo, k_hi, v_lo, v_hi = kr, kg, vr, vg
        kzs.append((jnp.where(lo, k_lo, 0.0).astype(BF16), jnp.where(lo, 0.0, k_hi).astype(BF16)))
        vxs.append((jnp.concatenate([jnp.where(lo, v_lo, 0.0).astype(BF16), ones_b], axis=1),
                    jnp.concatenate([jnp.where(lo, 0.0, v_hi).astype(BF16), ones_b], axis=1)))
        q4 = jnp.concatenate([q_ref[:, (hk * len(slabs) + sl) * LANES:(hk * len(slabs) + sl + 1) * LANES]
                              for sl in slabs], axis=0)
        q4s.append((q4.astype(F32) * (hd ** -0.5 * LOG2E)).astype(BF16))

    def qk(hk):
        return [_dot_nt(q4s[hk], kzs[hk][half]) for half in halves]

    scores = {0: qk(0)}
    for hk in kvs:
        if hk + 1 < SW_KV_HEADS:
            scores[hk + 1] = qk(hk + 1)
        o = None
        for half in halves:
            ps, es = [], []
            for sl in slabs:
                head = hk * SW_GROUP + 2 * sl + half
                sink = sink_ref[head] * LOG2E
                s = scores[hk][half][sl * w:(sl + 1) * w] + bias_ref[0, head]
                m = jnp.maximum(jnp.max(s, axis=-1, keepdims=True), sink)
                ps.append(jnp.exp2(s - m).astype(BF16))
                es.append(jnp.exp2(sink - m))
            r = _dot(jnp.concatenate(ps, axis=0), vxs[hk][half])
            part = r[:, :LANES] / (r[:, LANES:] + jnp.concatenate(es, axis=0))
            o = part if o is None else o + part
        for sl in slabs:
            c0 = (hk * len(slabs) + sl) * LANES
            z = z_refs[hk][:, sl * LANES:(sl + 1) * LANES].astype(F32)
            o_ref[:, c0:c0 + LANES] = (o[sl * w:(sl + 1) * w] * _silu(z)).astype(o_ref.dtype)


def _sliding_window_attention(proj, sinks, batch, seq):
    n = proj.shape[0]
    w = SW_WINDOW
    nb = seq // w
    kcb = SW_WIDTH // SW_KV_WIDTH
    zw = SW_WIDTH // SW_KV_HEADS
    zcb = (SW_WIDTH + 2 * SW_KV_WIDTH) // zw
    cur = lambda b, i: b * nb + i
    prev = lambda b, i: b * nb + jnp.maximum(i - 1, 0)
    in_specs = [
        pl.BlockSpec(memory_space=pltpu.SMEM),
        pl.BlockSpec((1, SW_HEADS, w, 2 * w), lambda b, i: (jnp.minimum(i, 1), 0, 0, 0)),
        pl.BlockSpec((w, SW_WIDTH), lambda b, i: (cur(b, i), 0)),
        pl.BlockSpec((w, SW_KV_WIDTH), lambda b, i: (cur(b, i), kcb)),
        pl.BlockSpec((w, SW_KV_WIDTH), lambda b, i: (prev(b, i), kcb)),
        pl.BlockSpec((w, SW_KV_WIDTH), lambda b, i: (cur(b, i), kcb + 1)),
        pl.BlockSpec((w, SW_KV_WIDTH), lambda b, i: (prev(b, i), kcb + 1)),
    ]
    for hk in range(SW_KV_HEADS):
        in_specs.append(pl.BlockSpec((w, zw), lambda b, i, hk=hk: (cur(b, i), zcb + hk)))
    return pl.pallas_call(
        _swa_kernel,
        grid=(batch, nb),
        in_specs=in_specs,
        out_specs=pl.BlockSpec((w, SW_WIDTH), lambda b, i: (cur(b, i), 0)),
        out_shape=jax.ShapeDtypeStruct((n, SW_WIDTH), BF16),
        compiler_params=pltpu.CompilerParams(
            dimension_semantics=("arbitrary", "arbitrary"), vmem_limit_bytes=VMEM_LIMIT),
        name="swa",
    )(sinks.astype(F32), _swa_bias_table(SW_HEADS), *([proj] * 9))


def kernel(x, c, norm_w, ada_w, ada_b, ev_w_in, ev_w_out, ev_lam_q1, ev_lam_k1, ev_lam_q2, ev_lam_k2, ev_diff_norm, ev_conv_w, ev_A_log, ev_dt_bias, ev_delta_norm, od_w_in, od_w_out, od_sinks, final_norm_w):
    batch, seq, d = x.shape
    depth = ada_w.shape[0]
    n = batch * seq
    x2 = x.reshape(n, d)

    mod = _ada_mod(c, ada_w, ada_b)
    tm_out = min(512, seq)
    tm_in = min(1024, seq)
    mod3s = [mod[layer].reshape(batch, 3, d) for layer in range(depth)]

    ev_w_in_t = jnp.swapaxes(ev_w_in, 1, 2)

    h2, tail = _prenorm(x2, norm_w[0], mod3s[0], ev_w_in_t, 0, seq, tm_out)
    for layer in range(depth):
        i = layer // 2
        last = layer == depth - 1
        next_norm_w = final_norm_w if last else norm_w[layer + 1]
        next_mod3 = None if last else mod3s[layer + 1]
        next_tail = (ev_w_in_t, (layer + 1) // 2) if (not last and layer % 2 == 1) else None
        if layer % 2 == 0:
            lam_init = 0.8 - 0.6 * math.exp(-0.3 * layer)
            proj = _inproj(h2, ev_w_in_t, i, EVEN_MAIN, tm_in, 1024, w_transposed=True)
            lamv = jnp.stack([ev_lam_q1[i], ev_lam_k1[i], ev_lam_q2[i], ev_lam_k2[i]]).astype(F32)
            ya = _diff_attention(proj, lamv, ev_diff_norm[i], batch, seq, lam_init, tq=min(512, seq))
            yb = _gated_deltanet(proj, tail, ev_conv_w[i], ev_A_log[i], ev_dt_bias[i], ev_delta_norm[i], batch, seq)
            ys, w_out = [ya, yb], ev_w_out
        else:
            proj = _inproj(h2, od_w_in, i, ODD_IN, tm_in, 1536)
            ys, w_out = [_sliding_window_attention(proj, od_sinks[i], batch, seq)], od_w_out
        x2, h2, tail = _outproj(ys, w_out, i, x2, mod3s[layer], next_norm_w, next_mod3, next_tail, seq, tm_out)
    return x2.reshape(batch, seq, d)
```

```python
import functools
import math

import jax
import jax.numpy as jnp
from jax import lax
from jax.experimental import pallas as pl
from jax.experimental.pallas import tpu as pltpu

F32 = jnp.float32
BF16 = jnp.bfloat16

D_MODEL = 2048
DEPTH = 4
RMS_EPS = 1e-6

DA_HEADS = 8
DA_QK_DIM = 64
DA_V_DIM = 128
DA_WIDTH = DA_HEADS * DA_V_DIM
GD_HEADS = 8
GD_DK = 128
GD_DV = 128
GD_WIDTH = GD_HEADS * GD_DV
GD_CONV = 4
GD_CHUNK = 64
GD_QKV = 3 * GD_WIDTH

EVEN_MAIN = 4 * DA_WIDTH + GD_QKV + GD_WIDTH
EVEN_IN = EVEN_MAIN + 2 * GD_HEADS

SW_HEADS = 32
SW_KV_HEADS = 4
SW_GROUP = SW_HEADS // SW_KV_HEADS
SW_HEAD_DIM = 64
SW_WIDTH = SW_HEADS * SW_HEAD_DIM
SW_KV_WIDTH = SW_KV_HEADS * SW_HEAD_DIM
SW_WINDOW = 128
ODD_IN = 2 * SW_WIDTH + 2 * SW_KV_WIDTH

LANES = 128
NEG_BIG = -1e30
LOG2E = math.log2(math.e)
OUT_ROW_CHUNKS = 2
VMEM_LIMIT = 56 * 1024 * 1024

HIGHEST = lax.Precision.HIGHEST


def _dot(a, b, precision=None):
    return jnp.dot(a, b, preferred_element_type=F32, precision=precision)


def _dot_nt(a, b, precision=None):
    return lax.dot_general(a, b, (((1,), (1,)), ((), ())), preferred_element_type=F32, precision=precision)


def _dot_tn(a, b, precision=None):
    return lax.dot_general(a, b, (((0,), (0,)), ((), ())), preferred_element_type=F32, precision=precision)


def _silu(x):
    return x * jax.nn.sigmoid(x)


def _alibi_slopes(n):
    return 2.0 ** (-8.0 * jnp.arange(1, n + 1, dtype=F32) / n)


def _ada_kernel(c_ref, w_ref, b_ref, o_ref):
    c = c_ref[...]
    cond = _silu(c).astype(BF16)
    w = w_ref[0].astype(BF16)
    o_ref[0] = _dot(cond, w) + b_ref[0]


def _ada_mod(c, ada_w, ada_b, tn=512):
    depth, d, n3 = ada_w.shape
    b = c.shape[0]
    return pl.pallas_call(
        _ada_kernel,
        grid=(depth, n3 // tn),
        in_specs=[
            pl.BlockSpec((b, d), lambda l, j: (0, 0)),
            pl.BlockSpec((1, d, tn), lambda l, j: (l, 0, j)),
            pl.BlockSpec((1, 1, tn), lambda l, j: (l, 0, j)),
        ],
        out_specs=pl.BlockSpec((1, b, tn), lambda l, j: (l, 0, j)),
        out_shape=jax.ShapeDtypeStruct((depth, b, n3), F32),
        compiler_params=pltpu.CompilerParams(
            dimension_semantics=("arbitrary", "arbitrary"), vmem_limit_bytes=VMEM_LIMIT),
        name="ada_mod",
    )(c, ada_w, ada_b.reshape(depth, 1, n3))


def _modulated_norm(x, nw, m):
    ms = jnp.mean(x * x, axis=-1, keepdims=True)
    y = x * lax.rsqrt(ms + RMS_EPS) * nw
    return (y * (1.0 + m[1:2, :]) + m[0:1, :]).astype(BF16)


def _tail_proj(h, wt_ref):
    w = wt_ref[0]
    row = lax.broadcasted_iota(jnp.int32, w.shape, 0)
    return _dot_nt(h, jnp.where(row < 2 * GD_HEADS, w, 0.0).astype(BF16))


def _tail_spec(layer_idx, d):
    return pl.BlockSpec((1, LANES, d), lambda i: (layer_idx, EVEN_MAIN // LANES, 0))


def _prenorm_kernel(x_ref, nw_ref, m_ref, wt_ref, h_ref, t_ref):
    h = _modulated_norm(x_ref[...], nw_ref[...], m_ref[0])
    h_ref[...] = h
    t_ref[...] = _tail_proj(h, wt_ref)


def _prenorm(x2, norm_w, mod3, ev_w_in, layer_idx, seq, tm):
    n, d = x2.shape
    per_b = seq // tm
    return pl.pallas_call(
        _prenorm_kernel,
        grid=(n // tm,),
        in_specs=[
            pl.BlockSpec((tm, d), lambda i: (i, 0)),
            pl.BlockSpec((1, d), lambda i: (0, 0)),
            pl.BlockSpec((1, 3, d), lambda i: (i // per_b, 0, 0)),
            _tail_spec(layer_idx, d),
        ],
        out_specs=[pl.BlockSpec((tm, d), lambda i: (i, 0)), pl.BlockSpec((tm, LANES), lambda i: (i, 0))],
        out_shape=[jax.ShapeDtypeStruct((n, d), BF16), jax.ShapeDtypeStruct((n, LANES), F32)],
        compiler_params=pltpu.CompilerParams(
            dimension_semantics=("arbitrary",), vmem_limit_bytes=VMEM_LIMIT),
        name="prenorm",
    )(x2, norm_w.reshape(1, d), mod3, ev_w_in)


def _inproj_kernel(h_ref, w_ref, o_ref, wb_scr, *, w_transposed):
    @pl.when(pl.program_id(1) == 0)
    def _():
        wb_scr[...] = w_ref[0].astype(BF16)

    mm = _dot_nt if w_transposed else _dot
    o_ref[...] = mm(h_ref[...], wb_scr[...]).astype(o_ref.dtype)


def _inproj(h2, w_all, layer_idx, nout, tm, tn, w_transposed=False):
    n, d = h2.shape
    if w_transposed:
        w_spec = pl.BlockSpec((1, tn, d), lambda j, i: (layer_idx, j, 0))
        w_tile = (tn, d)
    else:
        w_spec = pl.BlockSpec((1, d, tn), lambda j, i: (layer_idx, 0, j))
        w_tile = (d, tn)
    return pl.pallas_call(
        functools.partial(_inproj_kernel, w_transposed=w_transposed),
        grid=(nout // tn, n // tm),
        in_specs=[pl.BlockSpec((tm, d), lambda j, i: (i, 0)), w_spec],
        out_specs=pl.BlockSpec((tm, tn), lambda j, i: (i, j)),
        out_shape=jax.ShapeDtypeStruct((n, nout), BF16),
        scratch_shapes=[pltpu.VMEM(w_tile, BF16)],
        compiler_params=pltpu.CompilerParams(
            dimension_semantics=("arbitrary", "arbitrary"), vmem_limit_bytes=VMEM_LIMIT),
        name="inproj_t" if w_transposed else "inproj",
    )(h2, w_all)


def _outproj_kernel(*refs, n_y, final, has_tail):
    y_refs = refs[:n_y]
    w_ref, x_ref, m_ref, nw_ref = refs[n_y:n_y + 4]
    rest = list(refs[n_y + 4:])
    wb_scr = rest.pop()
    if not final:
        mn_ref = rest.pop(0)
        wt_ref = rest.pop(0) if has_tail else None
    o_ref = rest.pop(0)

    @pl.when(pl.program_id(0) == 0)
    def _():
        wb_scr[...] = w_ref[0].astype(BF16)

    gate = m_ref[0][2:3, :]
    rows = x_ref.shape[0]
    chunk = rows // OUT_ROW_CHUNKS
    for r0 in range(0, rows, chunk):
        rs = slice(r0, r0 + chunk)
        acc = None
        row = 0
        for y_ref in y_refs:
            kw = y_ref.shape[1]
            part = _dot(y_ref[rs, :], wb_scr[row:row + kw, :])
            acc = part if acc is None else acc + part
            row += kw
        xn = x_ref[rs, :] + gate * acc
        if final:
            ms = jnp.mean(xn * xn, axis=-1, keepdims=True)
            o_ref[rs, :] = xn * lax.rsqrt(ms + RMS_EPS) * nw_ref[...]
        else:
            o_ref[rs, :] = xn
            h = _modulated_norm(xn, nw_ref[...], mn_ref[0])
            rest[0][rs, :] = h
            if has_tail:
                rest[1][rs, :] = _tail_proj(h, wt_ref)


def _outproj(ys, w_all, layer_idx, x2, mod3, next_norm_w, next_mod3, next_tail, seq, tm):
    n, d = x2.shape
    per_b = seq // tm
    final = next_mod3 is None
    has_tail = next_tail is not None
    row_spec = pl.BlockSpec((tm, d), lambda i: (i, 0))
    mod_spec = pl.BlockSpec((1, 3, d), lambda i: (i // per_b, 0, 0))
    wshape = (1,) + w_all.shape[1:]
    in_specs = [pl.BlockSpec((tm, y.shape[1]), lambda i: (i, 0)) for y in ys]
    in_specs += [pl.BlockSpec(wshape, lambda i: (layer_idx, 0, 0), pipeline_mode=pl.Buffered(1)),
                 row_spec, mod_spec, pl.BlockSpec((1, d), lambda i: (0, 0))]
    args = list(ys) + [w_all, x2, mod3, next_norm_w.reshape(1, d)]
    if final:
        out_specs = row_spec
        out_shape = jax.ShapeDtypeStruct((n, d), F32)
    else:
        in_specs.append(mod_spec)
        args.append(next_mod3)
        out_specs = [row_spec, row_spec]
        out_shape = [jax.ShapeDtypeStruct((n, d), F32), jax.ShapeDtypeStruct((n, d), BF16)]
        if has_tail:
            in_specs.append(_tail_spec(next_tail[1], d))
            args.append(next_tail[0])
            out_specs.append(pl.BlockSpec((tm, LANES), lambda i: (i, 0)))
            out_shape.append(jax.ShapeDtypeStruct((n, LANES), F32))
    res = pl.pallas_call(
        functools.partial(_outproj_kernel, n_y=len(ys), final=final, has_tail=has_tail),
        grid=(n // tm,),
        in_specs=in_specs,
        out_specs=out_specs,
        out_shape=out_shape,
        scratch_shapes=[pltpu.VMEM(w_all.shape[1:], BF16)],
        compiler_params=pltpu.CompilerParams(
            dimension_semantics=("arbitrary",), vmem_limit_bytes=VMEM_LIMIT),
        name="outproj_final" if final else ("outproj_tail" if has_tail else "outproj"),
    )(*args)
    if final:
        return res, None, None
    return (res[0], res[1], res[2]) if has_tail else (res[0], res[1], None)


def _da_kernel(slope_ref, lam_ref, q_ref, k_ref, v_ref, z_ref, nw_ref, o_ref, *, tq, tk, nq_tiles, lam_init):
    slope2 = slope_ref[pl.program_id(1)] * LOG2E
    qi = pl.program_id(2)
    for nfull in range(nq_tiles):
        pl.when(qi == nfull)(functools.partial(
            _da_tile_program, nfull, slope2, lam_ref, q_ref, k_ref, v_ref, z_ref, nw_ref, o_ref,
            tq=tq, tk=tk, lam_init=lam_init))


def _da_tile_program(nfull, slope2, lam_ref, q_ref, k_ref, v_ref, z_ref, nw_ref, o_ref, *, tq, tk, lam_init):
    i0 = nfull * tq
    q = q_ref[...].astype(F32) * (DA_QK_DIM ** -0.5 * LOG2E)
    lane = lax.broadcasted_iota(jnp.int32, q.shape, 1)
    qmaps = (jnp.where(lane < DA_QK_DIM, q, 0.0).astype(BF16), jnp.where(lane >= DA_QK_DIM, q, 0.0).astype(BF16))

    def scores(ks, nk, q0):
        qst = jnp.concatenate([qm[q0:] for qm in qmaps], axis=0)
        return _dot_nt(k_ref[ks:ks + nk, :], qst)

    def update(s, ks, nk, q0, masked, m_all, l_all, acc_all):
        nq = tq - q0
        v = v_ref[ks:ks + nk, :]
        key_i = lax.broadcasted_iota(jnp.int32, (nk, LANES), 0)
        bias = slope2 * (key_i + (ks - i0)).astype(F32)
        s = s + jnp.concatenate([bias] * (2 * nq // LANES), axis=1)
        if masked:
            kj = lax.broadcasted_iota(jnp.int32, (nk, 2 * nq), 0) + (ks - i0)
            qc = (lax.broadcasted_iota(jnp.int32, (nk, 2 * nq), 1) & (nq - 1)) + q0
            s = jnp.where(kj <= qc, s, NEG_BIG)
        gather = lambda x: jnp.concatenate([x[:, q0:tq], x[:, tq + q0:]], axis=1)
        scatter = lambda x, new: jnp.concatenate([x[:, :q0], new[:, :nq], x[:, tq:tq + q0], new[:, nq:]], axis=1)
        m_prev = gather(m_all)
        m_new = jnp.maximum(m_prev, jnp.max(s, axis=0, keepdims=True))
        alpha = jnp.exp2(m_prev - m_new)
        p = jnp.exp2(s - m_new)
        l_new = alpha * gather(l_all) + jnp.sum(p, axis=0, keepdims=True)
        acc_new = alpha * gather(acc_all) + _dot_tn(v, p.astype(BF16))
        if q0 == 0:
            return m_new, l_new, acc_new
        return scatter(m_all, m_new), scatter(l_all, l_new), scatter(acc_all, acc_new)

    half = tk // 2
    tiles = [(t * tk, tk, 0, False) for t in range(nfull)]
    tiles += [(i0, half, 0, True), (i0 + half, half, tq // 2, True)]
    m = jnp.full((1, 2 * tq), NEG_BIG, F32)
    l = jnp.zeros((1, 2 * tq), F32)
    acc = jnp.zeros((DA_V_DIM, 2 * tq), F32)
    s_next = scores(*tiles[0][:3])
    for t, (ks, nk, q0, masked) in enumerate(tiles):
        s_cur = s_next
        if t + 1 < len(tiles):
            s_next = scores(*tiles[t + 1][:3])
        m, l, acc = update(s_cur, ks, nk, q0, masked, m, l, acc)

    lv = lam_ref[...]
    lam = (jnp.exp(jnp.sum(lv[0:1] * lv[1:2], axis=-1, keepdims=True))
           - jnp.exp(jnp.sum(lv[2:3] * lv[3:4], axis=-1, keepdims=True)) + lam_init)
    acc = acc / l
    o = (acc[:, :tq] - lam * acc[:, tq:]).T
    ms = jnp.mean(o * o, axis=-1, keepdims=True)
    o = o * lax.rsqrt(ms + RMS_EPS) * nw_ref[...] * (1.0 - lam_init)
    z = z_ref[...].astype(F32)
    o_ref[...] = (o * _silu(z)).astype(o_ref.dtype)


def _diff_attention(proj, lamv, diff_norm, batch, seq, lam_init, tq=512):
    n = proj.shape[0]
    nq = seq // tq
    hq = DA_HEADS
    return pl.pallas_call(
        functools.partial(_da_kernel, tq=tq, tk=tq, nq_tiles=nq, lam_init=lam_init),
        grid=(batch, hq, nq),
        in_specs=[
            pl.BlockSpec(memory_space=pltpu.SMEM),
            pl.BlockSpec((4, DA_QK_DIM), lambda b, h, i: (0, 0)),
            pl.BlockSpec((tq, LANES), lambda b, h, i: (b * nq + i, h)),
            pl.BlockSpec((seq, LANES), lambda b, h, i: (b, hq + h)),
            pl.BlockSpec((seq, LANES), lambda b, h, i: (b, 2 * hq + h)),
            pl.BlockSpec((tq, LANES), lambda b, h, i: (b * nq + i, 3 * hq + h)),
            pl.BlockSpec((1, DA_V_DIM), lambda b, h, i: (0, 0)),
        ],
        out_specs=pl.BlockSpec((tq, LANES), lambda b, h, i: (b * nq + i, h)),
        out_shape=jax.ShapeDtypeStruct((n, DA_WIDTH), BF16),
        compiler_params=pltpu.CompilerParams(
            dimension_semantics=("arbitrary", "arbitrary", "arbitrary"), vmem_limit_bytes=VMEM_LIMIT),
        name="diff_attn",
    )(_alibi_slopes(hq), lamv, proj, proj, proj, proj, diff_norm.reshape(1, DA_V_DIM))


GD_HALO = 8
GD_ROWS = 2 * GD_CHUNK


def _bdot(a, b):
    return _dot(a.astype(BF16), b.astype(BF16))


def _split3(x):
    b1 = x.astype(BF16)
    r1 = x - b1.astype(F32)
    b2 = r1.astype(BF16)
    b3 = (r1 - b2.astype(F32)).astype(BF16)
    return b1, b2, b3


def _gdn_kernel(q_ref, k_ref, v_ref, z_ref, t_ref, cw_ref, alog_ref, dtb_ref, nw_ref, o_ref,
                xbuf, s_scr):
    cs = GD_CHUNK
    rows = GD_ROWS
    heads = range(GD_HEADS)
    step = pl.program_id(1)

    @pl.when(step == 0)
    def _():
        xbuf[0:GD_HALO, :] = jnp.zeros((GD_HALO, GD_QKV), F32)
        s_scr[...] = jnp.zeros(s_scr.shape, F32)

    @pl.when(step > 0)
    def _():
        xbuf[0:GD_HALO, :] = xbuf[rows:rows + GD_HALO, :]

    xbuf[GD_HALO:GD_HALO + rows, 0:GD_WIDTH] = q_ref[...].astype(F32)
    xbuf[GD_HALO:GD_HALO + rows, GD_WIDTH:2 * GD_WIDTH] = k_ref[...].astype(F32)
    xbuf[GD_HALO:GD_HALO + rows, 2 * GD_WIDTH:3 * GD_WIDTH] = v_ref[...].astype(F32)

    def conv_silu(col0):
        xs = xbuf[:, col0:col0 + LANES]
        acc = None
        for kk in range(GD_CONV):
            shift = GD_CONV - 1 - kk
            xk = xs if shift == 0 else pltpu.roll(xs, shift, 0)
            term = cw_ref[kk:kk + 1, col0:col0 + LANES] * xk[GD_HALO:]
            acc = term if acc is None else acc + term
        return _silu(acc)

    def l2n(x):
        return x * lax.rsqrt(jnp.sum(x * x, axis=-1, keepdims=True) + RMS_EPS)

    tail = t_ref[...]
    beta_all = jax.nn.sigmoid(tail)
    g_all = -jnp.exp(alog_ref[...]) * jax.nn.softplus(tail + dtb_ref[...])

    ri = lax.broadcasted_iota(jnp.int32, (rows, rows), 0)
    ci = lax.broadcasted_iota(jnp.int32, (rows, rows), 1)
    same = (ri // cs) == (ci // cs)
    tril = same & (ri >= ci)
    strict = same & (ri > ci)
    tril_b = jnp.where(tril, 1.0, 0.0).astype(BF16)
    triu_b = jnp.where(same & (ri <= ci), 1.0, 0.0).astype(BF16)
    eye_f = jnp.where(ri == ci, 1.0, 0.0)

    gcs = _dot(tril_b, jnp.concatenate(_split3(g_all), axis=1))
    gc_all = gcs[:, :LANES] + gcs[:, LANES:2 * LANES] + gcs[:, 2 * LANES:]
    gts = _dot(jnp.concatenate(_split3(g_all.T), axis=0), triu_b)
    gc_t = gts[:LANES] + gts[LANES:2 * LANES] + gts[2 * LANES:]

    qs, ks, vs, bcols, gcws, gcbs, egcs = [], [], [], [], [], [], []
    for h in heads:
        qs.append(l2n(conv_silu(h * GD_DK)) * (GD_DK ** -0.5))
        ks.append(l2n(conv_silu(GD_WIDTH + h * GD_DK)))
        vs.append(conv_silu(2 * GD_WIDTH + h * GD_DV))
        bcols.append(beta_all[:, h:h + 1])
        gcw = jnp.broadcast_to(gc_all[:, GD_HEADS + h:GD_HEADS + h + 1], (rows, rows))
        gcws.append(gcw)
        gcbs.append(gcw[:, :LANES])
        egcs.append(jnp.exp(gcw[:, :LANES]))

    kqs = [_dot_nt(jnp.concatenate([qs[h], ks[h]], axis=0).astype(BF16), ks[h].astype(BF16)) for h in heads]

    m_lows, qkms = [], []
    for h in heads:
        dmat = gcws[h] - gc_t[GD_HEADS + h:GD_HEADS + h + 1, :]
        decay = jnp.exp(jnp.where(tril, dmat, NEG_BIG))
        m_lows.append(jnp.where(strict, bcols[h] * kqs[h][rows:] * decay, 0.0))
        qkms.append(jnp.where(tril, kqs[h][:rows] * decay, 0.0))

    blk = lambda sz: (ri // sz) == (ci // sz)
    tinvs = [eye_f - jnp.where(blk(2), m_lows[h], 0.0) for h in heads]
    sz = 4
    while sz <= cs:
        join = blk(sz) & jnp.logical_not(blk(sz // 2))
        tcs = [_bdot(tinvs[h], jnp.where(join, m_lows[h], 0.0)) for h in heads]
        tinvs = [tinvs[h] - _bdot(tcs[h], tinvs[h]) for h in heads]
        sz *= 2

    sols = [_bdot(tinvs[h], jnp.concatenate([bcols[h] * vs[h], bcols[h] * ks[h] * egcs[h]], axis=1)) for h in heads]

    states = [s_scr[h] for h in heads]
    v_news = [[] for _ in heads]
    o_inters = [[] for _ in heads]
    for c in range(rows // cs):
        r0 = c * cs
        for h in heads:
            s = states[h]
            gcb_c = gcbs[h][r0:r0 + cs]
            gl = gcb_c[cs - 1:cs, :]
            v_new = sols[h][r0:r0 + cs, :GD_DV] - _bdot(sols[h][r0:r0 + cs, GD_DV:], s)
            o_inters[h].append(_bdot(qs[h][r0:r0 + cs] * egcs[h][r0:r0 + cs], s))
            kd = ks[h][r0:r0 + cs] * jnp.exp(gl - gcb_c)
            states[h] = s * jnp.exp(gl) + _dot_tn(kd.astype(BF16), v_new.astype(BF16))
            v_news[h].append(v_new)

    for h in heads:
        s_scr[h] = states[h]
        o = jnp.concatenate(o_inters[h], axis=0) + _bdot(qkms[h], jnp.concatenate(v_news[h], axis=0))
        ms = jnp.mean(o * o, axis=-1, keepdims=True)
        o = o * lax.rsqrt(ms + RMS_EPS) * nw_ref[...]
        z = z_ref[:, h * GD_DV:(h + 1) * GD_DV].astype(F32)
        o_ref[:, h * GD_DV:(h + 1) * GD_DV] = (o * _silu(z)).astype(o_ref.dtype)


def _gated_deltanet(proj, tail, conv_w, a_log, dt_bias, delta_norm, batch, seq):
    n = proj.shape[0]
    cs = GD_ROWS
    nc = seq // cs
    pad = jnp.zeros((GD_HEADS,), F32)
    alog_row = jnp.concatenate([pad, a_log.astype(F32), jnp.zeros((LANES - 2 * GD_HEADS,), F32)]).reshape(1, LANES)
    dtb_row = jnp.concatenate([pad, dt_bias.astype(F32), jnp.zeros((LANES - 2 * GD_HEADS,), F32)]).reshape(1, LANES)
    cb = (4 * DA_WIDTH) // GD_WIDTH
    return pl.pallas_call(
        _gdn_kernel,
        grid=(batch, nc),
        in_specs=[
            pl.BlockSpec((cs, GD_WIDTH), lambda b, c: (b * nc + c, cb)),
            pl.BlockSpec((cs, GD_WIDTH), lambda b, c: (b * nc + c, cb + 1)),
            pl.BlockSpec((cs, GD_WIDTH), lambda b, c: (b * nc + c, cb + 2)),
            pl.BlockSpec((cs, GD_WIDTH), lambda b, c: (b * nc + c, cb + 3)),
            pl.BlockSpec((cs, LANES), lambda b, c: (b * nc + c, 0)),
            pl.BlockSpec((GD_CONV, GD_QKV), lambda b, c: (0, 0)),
            pl.BlockSpec((1, LANES), lambda b, c: (0, 0)),
            pl.BlockSpec((1, LANES), lambda b, c: (0, 0)),
            pl.BlockSpec((1, GD_DV), lambda b, c: (0, 0)),
        ],
        out_specs=pl.BlockSpec((cs, GD_WIDTH), lambda b, c: (b * nc + c, 0)),
        out_shape=jax.ShapeDtypeStruct((n, GD_WIDTH), BF16),
        scratch_shapes=[
            pltpu.VMEM((GD_HALO + cs, GD_QKV), F32),
            pltpu.VMEM((GD_HEADS, GD_DK, GD_DV), F32),
        ],
        compiler_params=pltpu.CompilerParams(
            dimension_semantics=("arbitrary", "arbitrary"), vmem_limit_bytes=VMEM_LIMIT),
        name="gated_deltanet",
    )(proj, proj, proj, proj, tail, conv_w, alog_row, dtb_row, delta_norm.reshape(1, GD_DV))


def _swa_bias_table(nheads):
    w = SW_WINDOW
    qi = jnp.arange(w)[:, None]
    kj = jnp.arange(2 * w)[None, :]
    dist = qi + w - kj
    in_window = (dist >= 0) & (dist < w)
    bias = -_alibi_slopes(nheads)[:, None, None] * dist.astype(F32)[None] * LOG2E
    general = jnp.where(in_window[None], bias, NEG_BIG)
    first = jnp.where((in_window & (kj >= w))[None], bias, NEG_BIG)
    return jnp.stack([first, general])


def _swa_kernel(sink_ref, bias_ref, q_ref, kc_ref, kp_ref, vc_ref, vp_ref,
                z0_ref, z1_ref, z2_ref, z3_ref, o_ref):
    w = SW_WINDOW
    hd = SW_HEAD_DIM
    z_refs = (z0_ref, z1_ref, z2_ref, z3_ref)

    kband = jnp.concatenate([kp_ref[...], kc_ref[...]], axis=0).astype(F32)
    vband = jnp.concatenate([vp_ref[...], vc_ref[...]], axis=0).astype(F32)

    lane = lax.broadcasted_iota(jnp.int32, (2 * w, LANES), 1)
    lo = lane < hd
    slabs = range(SW_GROUP * hd // LANES)
    kvs = range(SW_KV_HEADS)
    halves = range(2)
    ones_b = jnp.ones((2 * w, LANES), BF16)

    q4s, kzs, vxs = [], [], []
    for hk in kvs:
        g0 = (hk // 2) * LANES
        kg = kband[:, g0:g0 + LANES]
        vg = vband[:, g0:g0 + LANES]
        kr = pltpu.roll(kg, hd, 1)
        vr = pltpu.roll(vg, hd, 1)
        if hk % 2 == 0:
            k_lo, k_hi, v_lo, v_hi = kg, kr, vg, vr
        else:
            k_lo, k_hi, v_lo, v_hi = kr, kg, vr, vg
        kzs.append((jnp.where(lo, k_lo, 0.0).astype(BF16), jnp.where(lo, 0.0, k_hi).astype(BF16)))
        vxs.append((jnp.concatenate([jnp.where(lo, v_lo, 0.0).astype(BF16), ones_b], axis=1),
                    jnp.concatenate([jnp.where(lo, 0.0, v_hi).astype(BF16), ones_b], axis=1)))
        q4 = jnp.concatenate([q_ref[:, (hk * len(slabs) + sl) * LANES:(hk * len(slabs) + sl + 1) * LANES]
                              for sl in slabs], axis=0)
        q4s.append((q4.astype(F32) * (hd ** -0.5 * LOG2E)).astype(BF16))

    def qk(hk):
        return [_dot_nt(q4s[hk], kzs[hk][half]) for half in halves]

    scores = {0: qk(0)}
    for hk in kvs:
        if hk + 1 < SW_KV_HEADS:
            scores[hk + 1] = qk(hk + 1)
        o = None
        for half in halves:
            ps, es = [], []
            for sl in slabs:
                head = hk * SW_GROUP + 2 * sl + half
                sink = sink_ref[head] * LOG2E
                s = scores[hk][half][sl * w:(sl + 1) * w] + bias_ref[0, head]
                m = jnp.maximum(jnp.max(s, axis=-1, keepdims=True), sink)
                ps.append(jnp.exp2(s - m).astype(BF16))
                es.append(jnp.exp2(sink - m))
            r = _dot(jnp.concatenate(ps, axis=0), vxs[hk][half])
            part = r[:, :LANES] / (r[:, LANES:] + jnp.concatenate(es, axis=0))
            o = part if o is None else o + part
        for sl in slabs:
            c0 = (hk * len(slabs) + sl) * LANES
            z = z_refs[hk][:, sl * LANES:(sl + 1) * LANES].astype(F32)
            o_ref[:, c0:c0 + LANES] = (o[sl * w:(sl + 1) * w] * _silu(z)).astype(o_ref.dtype)


def _sliding_window_attention(proj, sinks, batch, seq):
    n = proj.shape[0]
    w = SW_WINDOW
    nb = seq // w
    kcb = SW_WIDTH // SW_KV_WIDTH
    zw = SW_WIDTH // SW_KV_HEADS
    zcb = (SW_WIDTH + 2 * SW_KV_WIDTH) // zw
    cur = lambda b, i: b * nb + i
    prev = lambda b, i: b * nb + jnp.maximum(i - 1, 0)
    in_specs = [
        pl.BlockSpec(memory_space=pltpu.SMEM),
        pl.BlockSpec((1, SW_HEADS, w, 2 * w), lambda b, i: (jnp.minimum(i, 1), 0, 0, 0)),
        pl.BlockSpec((w, SW_WIDTH), lambda b, i: (cur(b, i), 0)),
        pl.BlockSpec((w, SW_KV_WIDTH), lambda b, i: (cur(b, i), kcb)),
        pl.BlockSpec((w, SW_KV_WIDTH), lambda b, i: (prev(b, i), kcb)),
        pl.BlockSpec((w, SW_KV_WIDTH), lambda b, i: (cur(b, i), kcb + 1)),
        pl.BlockSpec((w, SW_KV_WIDTH), lambda b, i: (prev(b, i), kcb + 1)),
    ]
    for hk in range(SW_KV_HEADS):
        in_specs.append(pl.BlockSpec((w, zw), lambda b, i, hk=hk: (cur(b, i), zcb + hk)))
    return pl.pallas_call(
        _swa_kernel,
        grid=(batch, nb),
        in_specs=in_specs,
        out_specs=pl.BlockSpec((w, SW_WIDTH), lambda b, i: (cur(b, i), 0)),
        out_shape=jax.ShapeDtypeStruct((n, SW_WIDTH), BF16),
        compiler_params=pltpu.CompilerParams(
            dimension_semantics=("arbitrary", "arbitrary"), vmem_limit_bytes=VMEM_LIMIT),
        name="swa",
    )(sinks.astype(F32), _swa_bias_table(SW_HEADS), *([proj] * 9))


def kernel(x, c, norm_w, ada_w, ada_b, ev_w_in, ev_w_out, ev_lam_q1, ev_lam_k1, ev_lam_q2, ev_lam_k2, ev_diff_norm, ev_conv_w, ev_A_log, ev_dt_bias, ev_delta_norm, od_w_in, od_w_out, od_sinks, final_norm_w):
    batch, seq, d = x.shape
    depth = ada_w.shape[0]
    n = batch * seq
    x2 = x.reshape(n, d)

    mod = _ada_mod(c, ada_w, ada_b)
    tm_out = min(512, seq)
    tm_in = min(1024, seq)
    mod3s = [mod[layer].reshape(batch, 3, d) for layer in range(depth)]

    ev_w_in_t = jnp.swapaxes(ev_w_in, 1, 2)

    h2, tail = _prenorm(x2, norm_w[0], mod3s[0], ev_w_in_t, 0, seq, tm_out)
    for layer in range(depth):
        i = layer // 2
        last = layer == depth - 1
        next_norm_w = final_norm_w if last else norm_w[layer + 1]
        next_mod3 = None if last else mod3s[layer + 1]
        next_tail = (ev_w_in_t, (layer + 1) // 2) if (not last and layer % 2 == 1) else None
        if layer % 2 == 0:
            lam_init = 0.8 - 0.6 * math.exp(-0.3 * layer)
            proj = _inproj(h2, ev_w_in_t, i, EVEN_MAIN, 2 * tm_in, 1024, w_transposed=True)
            lamv = jnp.stack([ev_lam_q1[i], ev_lam_k1[i], ev_lam_q2[i], ev_lam_k2[i]]).astype(F32)
            ya = _diff_attention(proj, lamv, ev_diff_norm[i], batch, seq, lam_init, tq=min(512, seq))
            yb = _gated_deltanet(proj, tail, ev_conv_w[i], ev_A_log[i], ev_dt_bias[i], ev_delta_norm[i], batch, seq)
            ys, w_out = [ya, yb], ev_w_out
        else:
            proj = _inproj(h2, od_w_in, i, ODD_IN, tm_in, 1536)
            ys, w_out = [_sliding_window_attention(proj, od_sinks[i], batch, seq)], od_w_out
        x2, h2, tail = _outproj(ys, w_out, i, x2, mod3s[layer], next_norm_w, next_mod3, next_tail, seq, tm_out)
    return x2.reshape(batch, seq, d)
```
